```python
import math
import jax, jax.numpy as jnp
from jax import lax
import numpy as np

D_MODEL = 1024
BATCH = 8
SEQ = 2048
DEPTH = 1
DEC_BATCH = 128
DEC_SEQ = 4
PAST_LEN = 16384
PAGE_SIZE = 128

D_SSD = D_MODEL
SSD_HEAD_DIM = 64
SSD_HEADS = D_SSD // SSD_HEAD_DIM
SSD_GROUPS = 2
SSD_STATE = 128
CONV_W = 4
SSD_CHUNK = 128
CONV_DIM = D_SSD + 2 * SSD_GROUPS * SSD_STATE
D_HGRN = D_MODEL
HGRN_DK = 128
HGRN_HEADS = D_HGRN // HGRN_DK
HGRN_DV = D_HGRN // HGRN_HEADS
HGRN_CHUNK = 16
D_MIX = D_SSD + D_HGRN
PROJ_DIM = D_SSD + CONV_DIM + SSD_HEADS + 4 * D_HGRN
N_EXPERTS = 32
TOP_K = 4
D_FF = D_MODEL
SWIGLU_LIMIT = 7.0
SWIGLU_ALPHA = 1.702
MOE_BLOCK = 256
EPS = 1e-6

kernel_name = 'hybrid_ssd_hgrn2_moe_step'


def rmsnorm(x, w):
    xf = x.astype(jnp.float32)
    y = xf * lax.rsqrt(jnp.mean(xf * xf, axis=-1, keepdims=True) + EPS)
    return (y * w.astype(jnp.float32)).astype(x.dtype)


def to_chunks(a, L):
    Bsz, T = a.shape[:2]
    return jnp.swapaxes(a.reshape(Bsz, T // L, L, *a.shape[2:]), 0, 1)


def from_chunks(a):
    nc, Bsz, L = a.shape[:3]
    return jnp.swapaxes(a, 0, 1).reshape(Bsz, nc * L, *a.shape[3:])


def causal_conv(u, buf, w, b):
    T = u.shape[1]
    ext = jnp.concatenate([buf.astype(u.dtype), u], axis=1)
    out = b
    for k in range(CONV_W):
        out = out + ext[:, k:k + T] * w[k]
    return out, ext[:, T:]


def ssd_scan(x, dt, Bm, Cm, A, h0):
    T = x.shape[1]
    L = math.gcd(T, SSD_CHUNK)
    rep = SSD_HEADS // SSD_GROUPS
    Bh = jnp.repeat(Bm, rep, axis=2)
    Ch = jnp.repeat(Cm, rep, axis=2)
    a = dt * A
    mask = jnp.tril(jnp.ones((L, L), dtype=bool))

    def step(h, inp):
        xc, dtc, ac, bc, cc = inp
        acum = jnp.cumsum(ac, axis=1)
        seg = acum[:, :, None, :] - acum[:, None, :, :]
        decay = jnp.exp(jnp.where(mask[None, :, :, None], seg, -jnp.inf))
        scores = jnp.einsum('blhn,bshn->blsh', cc, bc) * decay * dtc[:, None]
        y = jnp.einsum('blsh,bshp->blhp', scores, xc)
        y = y + jnp.einsum('blhn,bhpn->blhp', cc, h) * jnp.exp(acum)[..., None]
        tail = jnp.exp(acum[:, -1:] - acum) * dtc
        h = (jnp.exp(acum[:, -1])[:, :, None, None] * h
             + jnp.einsum('blhn,blhp->bhpn', bc * tail[..., None], xc))
        return h, y

    hT, ys = lax.scan(step, h0, (to_chunks(x, L), to_chunks(dt, L), to_chunks(a, L),
                                 to_chunks(Bh, L), to_chunks(Ch, L)))
    return from_chunks(ys), hT


def hgrn2_scan(q, k, v, logf, S0):
    T = q.shape[1]
    L = math.gcd(T, HGRN_CHUNK)
    mask = jnp.tril(jnp.ones((L, L), dtype=bool))

    def step(S, inp):
        qc, kc, vc, gc = inp
        b = jnp.cumsum(gc, axis=1)
        qt = qc * jnp.exp(b)
        kt = kc * jnp.exp(-b)
        A = jnp.where(mask, jnp.einsum('blhk,bshk->bhls', qt, kt), 0.0)
        o = (jnp.einsum('bhls,bshv->blhv', A, vc)
             + jnp.einsum('blhk,bhkv->blhv', qt, S))
        b_last = b[:, -1]
        kend = kc * jnp.exp(b_last[:, None] - b)
        S = jnp.exp(b_last)[..., None] * S + jnp.einsum('blhk,blhv->bhkv', kend, vc)
        return S, o

    ST, os_ = lax.scan(step, S0, (to_chunks(q, L), to_chunks(k, L), to_chunks(v, L),
                                  to_chunks(logf, L)))
    return from_chunks(os_), ST


def mixer(h, conv_buf, ssm_h, hgrn_S, lb, w_in, conv_w, conv_b, dt_bias, A_log, D_skip,
          ssd_norm_w, hgrn_norm_w, w_out):
    f32 = jnp.float32
    Bsz, T, _ = h.shape
    proj = h @ w_in
    cuts = [int(c) for c in np.cumsum([D_SSD, CONV_DIM, SSD_HEADS, D_HGRN, D_HGRN, D_HGRN])]
    z, xbc, dt_raw, hq, hf, hi, hg = jnp.split(proj, cuts, axis=-1)
    xbc, new_conv = causal_conv(xbc, conv_buf, conv_w, conv_b)
    xbc = jax.nn.silu(xbc)
    xs, Bm, Cm = jnp.split(xbc, [D_SSD, D_SSD + SSD_GROUPS * SSD_STATE], axis=-1)
    xs = xs.reshape(Bsz, T, SSD_HEADS, SSD_HEAD_DIM).astype(f32)
    Bm = Bm.reshape(Bsz, T, SSD_GROUPS, SSD_STATE).astype(f32)
    Cm = Cm.reshape(Bsz, T, SSD_GROUPS, SSD_STATE).astype(f32)
    dt = jax.nn.softplus(dt_raw.astype(f32) + dt_bias.astype(f32))
    A = -jnp.exp(A_log.astype(f32))
    y, new_h = ssd_scan(xs, dt, Bm, Cm, A, ssm_h.astype(f32))
    y = y + D_skip.astype(f32)[:, None] * xs
    yz = y.reshape(Bsz, T, D_SSD) * jax.nn.silu(z.astype(f32))
    y_ssd = rmsnorm(yz.reshape(Bsz, T, SSD_GROUPS, D_SSD // SSD_GROUPS),
                    ssd_norm_w.reshape(SSD_GROUPS, D_SSD // SSD_GROUPS)).reshape(Bsz, T, D_SSD)
    q = jax.nn.silu(hq.astype(f32)).reshape(Bsz, T, HGRN_HEADS, HGRN_DK)
    lbh = lb.reshape(HGRN_HEADS, HGRN_DK)
    f = lbh + (1.0 - lbh) * jax.nn.sigmoid(hf.astype(f32).reshape(Bsz, T, HGRN_HEADS, HGRN_DK))
    v = hi.astype(f32).reshape(Bsz, T, HGRN_HEADS, HGRN_DV)
    o, new_S = hgrn2_scan(q, 1.0 - f, v, jnp.log(f), hgrn_S.astype(f32))
    o = rmsnorm(o, hgrn_norm_w.reshape(HGRN_HEADS, HGRN_DV)).reshape(Bsz, T, D_HGRN)
    o = o * jax.nn.silu(hg.astype(f32))
    mix = jnp.concatenate([y_ssd, o], axis=-1).astype(h.dtype)
    return mix @ w_out, new_conv, new_h, new_S


def moe(h, w_router, b_router, w_gate, b_gate, w_up, b_up, w_down, b_down):
    shp = h.shape
    xt = h.reshape(-1, D_MODEL)
    T = xt.shape[0]
    logits = (xt @ w_router).astype(jnp.float32) + b_router.astype(jnp.float32)
    top_v, top_i = lax.top_k(logits, TOP_K)
    gates = jax.nn.softmax(top_v, axis=-1)
    flat_e = top_i.reshape(-1)
    flat_tok = jnp.repeat(jnp.arange(T, dtype=jnp.int32), TOP_K)
    flat_g = gates.reshape(-1)
    order = jnp.argsort(flat_e)
    e_sorted = flat_e[order]
    counts = jnp.bincount(flat_e, length=N_EXPERTS)
    start = jnp.cumsum(counts) - counts
    padded = (counts + MOE_BLOCK - 1) // MOE_BLOCK * MOE_BLOCK
    pad_end = jnp.cumsum(padded)
    pad_start = pad_end - padded
    dest = pad_start[e_sorted] + (jnp.arange(T * TOP_K, dtype=jnp.int32) - start[e_sorted])
    n_blocks = -(-(T * TOP_K) // MOE_BLOCK) + N_EXPERTS
    R = n_blocks * MOE_BLOCK
    buf_tok = jnp.zeros((R,), jnp.int32).at[dest].set(flat_tok[order])
    buf_g = jnp.zeros((R,), jnp.float32).at[dest].set(flat_g[order])
    block_e = jnp.minimum(jnp.searchsorted(pad_end, jnp.arange(n_blocks) * MOE_BLOCK, side='right'),
                          N_EXPERTS - 1)

    def expert_block(args):
        tok, e = args
        xb = xt[tok]
        g = xb @ w_gate[e] + b_gate[e]
        u = xb @ w_up[e] + b_up[e]
        g = jnp.minimum(g, SWIGLU_LIMIT)
        u = jnp.clip(u, -SWIGLU_LIMIT, SWIGLU_LIMIT)
        act = g * jax.nn.sigmoid(SWIGLU_ALPHA * g) * (u + 1.0)
        return act @ w_down[e] + b_down[e]

    out = lax.map(expert_block, (buf_tok.reshape(n_blocks, MOE_BLOCK), block_e))
    contrib = (out.reshape(R, D_MODEL) * buf_g[:, None]).astype(xt.dtype)
    y = jnp.zeros_like(xt).at[buf_tok].add(contrib)
    return y.reshape(shp)


def block(x, conv_buf, ssm_h, hgrn_S, lb, norm1_w, w_in, conv_w, conv_b, dt_bias, A_log, D_skip,
          ssd_norm_w, hgrn_norm_w, w_out, norm2_w, w_router, b_router, w_gate, b_gate,
          w_up, b_up, w_down, b_down):
    mix, new_conv, new_h, new_S = mixer(rmsnorm(x, norm1_w), conv_buf, ssm_h, hgrn_S, lb, w_in,
                                        conv_w, conv_b, dt_bias, A_log, D_skip, ssd_norm_w,
                                        hgrn_norm_w, w_out)
    x = x + mix
    x = x + moe(rmsnorm(x, norm2_w), w_router, b_router, w_gate, b_gate, w_up, b_up, w_down, b_down)
    return x, new_conv, new_h, new_S


def setup_inputs(seed: int = 0) -> dict:
    key = jax.random.key(seed)
    ks = jax.random.split(key, 32)
    f32 = jnp.float32
    nrm = lambda k, shape, s: (jax.random.normal(k, shape, f32) * s)
    dt = jnp.exp(jax.random.uniform(ks[10], (DEPTH, SSD_HEADS), f32, math.log(1e-3), math.log(1e-1)))
    return {
        'x_prompt': nrm(ks[0], (BATCH, SEQ, D_MODEL), 1.0),
        'x_sample': nrm(ks[1], (DEC_BATCH, DEC_SEQ, D_MODEL), 1.0),
        'state_conv': nrm(ks[2], (DEPTH, DEC_BATCH, CONV_W - 1, CONV_DIM), 1.0),
        'state_ssm': nrm(ks[3], (DEPTH, DEC_BATCH, SSD_HEADS, SSD_HEAD_DIM, SSD_STATE), 0.1),
        'state_hgrn': nrm(ks[4], (DEPTH, DEC_BATCH, HGRN_HEADS, HGRN_DK, HGRN_DV), 0.3),
        'norm1_w': 1.0 + nrm(ks[5], (DEPTH, D_MODEL), 0.02),
        'w_in': nrm(ks[6], (DEPTH, D_MODEL, PROJ_DIM), D_MODEL ** -0.5),
        'conv_w': nrm(ks[7], (DEPTH, CONV_W, CONV_DIM), CONV_W ** -0.5),
        'conv_b': nrm(ks[8], (DEPTH, CONV_DIM), 0.01),
        'dt_bias': dt + jnp.log(-jnp.expm1(-dt)),
        'A_log': jnp.log(jax.random.uniform(ks[11], (DEPTH, SSD_HEADS), f32, 1.0, 16.0)),
        'D_skip': 1.0 + nrm(ks[12], (DEPTH, SSD_HEADS), 0.1),
        'ssd_norm_w': 1.0 + nrm(ks[13], (DEPTH, D_SSD), 0.02),
        'hgrn_lower_bound': nrm(ks[14], (DEPTH + 1, D_HGRN), 0.1),
        'hgrn_norm_w': 1.0 + nrm(ks[15], (DEPTH, D_HGRN), 0.02),
        'w_out': nrm(ks[16], (DEPTH, D_MIX, D_MODEL), D_MIX ** -0.5),
        'norm2_w': 1.0 + nrm(ks[17], (DEPTH, D_MODEL), 0.02),
        'w_router': nrm(ks[18], (DEPTH, D_MODEL, N_EXPERTS), D_MODEL ** -0.5),
        'b_router': nrm(ks[19], (DEPTH, N_EXPERTS), 0.01),
        'w_gate': nrm(ks[20], (DEPTH, N_EXPERTS, D_MODEL, D_FF), D_MODEL ** -0.5),
        'b_gate': nrm(ks[21], (DEPTH, N_EXPERTS, D_FF), 0.01),
        'w_up': nrm(ks[22], (DEPTH, N_EXPERTS, D_MODEL, D_FF), D_MODEL ** -0.5),
        'b_up': nrm(ks[23], (DEPTH, N_EXPERTS, D_FF), 0.01),
        'w_down': nrm(ks[24], (DEPTH, N_EXPERTS, D_FF, D_MODEL), D_FF ** -0.5),
        'b_down': nrm(ks[25], (DEPTH, N_EXPERTS, D_MODEL), 0.01),
        'norm_f_w': 1.0 + nrm(ks[26], (D_MODEL,), 0.02),
    }


def reference(x_prompt, x_sample, state_conv, state_ssm, state_hgrn, norm1_w, w_in, conv_w, conv_b,
              dt_bias, A_log, D_skip, ssd_norm_w, hgrn_lower_bound, hgrn_norm_w, w_out, norm2_w,
              w_router, b_router, w_gate, b_gate, w_up, b_up, w_down, b_down, norm_f_w):
    lb_all = jnp.cumsum(jax.nn.softmax(hgrn_lower_bound.astype(jnp.float32), axis=0), axis=0)
    Bp = x_prompt.shape[0]
    xp, xs = x_prompt, x_sample
    conv_p, ssm_p, hgrn_p, conv_s, ssm_s, hgrn_s = [], [], [], [], [], []
    for l in range(DEPTH):
        params = (lb_all[l], norm1_w[l], w_in[l], conv_w[l], conv_b[l], dt_bias[l], A_log[l],
                  D_skip[l], ssd_norm_w[l], hgrn_norm_w[l], w_out[l], norm2_w[l], w_router[l],
                  b_router[l], w_gate[l], b_gate[l], w_up[l], b_up[l], w_down[l], b_down[l])
        xp, c, h, S = block(xp,
                            jnp.zeros((Bp, CONV_W - 1, CONV_DIM), xp.dtype),
                            jnp.zeros((Bp, SSD_HEADS, SSD_HEAD_DIM, SSD_STATE), state_ssm.dtype),
                            jnp.zeros((Bp, HGRN_HEADS, HGRN_DK, HGRN_DV), state_hgrn.dtype),
                            *params)
        conv_p.append(c); ssm_p.append(h); hgrn_p.append(S)
        xs, c, h, S = block(xs, state_conv[l], state_ssm[l], state_hgrn[l], *params)
        conv_s.append(c); ssm_s.append(h); hgrn_s.append(S)
    y_prompt = rmsnorm(xp, norm_f_w)
    y_sample = rmsnorm(xs, norm_f_w)
    return (y_prompt, y_sample,
            jnp.stack(conv_p).astype(state_conv.dtype),
            jnp.stack(ssm_p).astype(state_ssm.dtype),
            jnp.stack(hgrn_p).astype(state_hgrn.dtype),
            jnp.stack(conv_s).astype(state_conv.dtype),
            jnp.stack(ssm_s).astype(state_ssm.dtype),
            jnp.stack(hgrn_s).astype(state_hgrn.dtype))
```

```python
import functools

import jax
import jax.numpy as jnp
from jax import lax
from jax.experimental import pallas as pl
from jax.experimental.pallas import tpu as pltpu

F32 = jnp.float32
BF16 = jnp.bfloat16
I32 = jnp.int32
HI = lax.Precision.HIGHEST

D_MODEL = 1024
SSD_HEADS = 16
SSD_P = 64
SSD_N = 128
CONV_DIM = 1536
CONV_W = 4
HGRN_HEADS = 8
HGRN_D = 128
N_EXPERTS = 32
TOP_K = 4
MOE_BLOCK = 256
SWIGLU_LIMIT = 7.0
SWIGLU_ALPHA = 1.702
EPS = 1e-6

Z0 = 0
X0 = 1024
Q0 = 2560
F0 = 3584
I0 = 4608
G0 = 5632
DT0 = 6656
PW = 6784
LANE = 128
HALO = 8
SSD_CHUNK = 128
HGRN_CHUNK = 16
VMEM_LIMIT = 56 * 1024 * 1024


def _silu(x):
    return x * jax.nn.sigmoid(x)


def _softplus(x):
    return jnp.maximum(x, 0.0) + jnp.log1p(jnp.exp(-jnp.abs(x)))


def _dot(a, b):
    return jnp.dot(a, b, preferred_element_type=F32)


def _dot_nt(a, b, precision=None):
    return lax.dot_general(a, b, (((1,), (1,)), ((), ())), precision=precision, preferred_element_type=F32)


def _dot_tn(a, b, precision=None):
    return lax.dot_general(a, b, (((0,), (0,)), ((), ())), precision=precision, preferred_element_type=F32)


def _in_proj(x, n1_ref, win_ref, pj, r0):
    rows = x.shape[0]
    ms = jnp.mean(x * x, axis=-1, keepdims=True)
    h = (x * lax.rsqrt(ms + EPS) * n1_ref[...]).astype(BF16)
    for c0 in range(0, PW, 512):
        c1 = min(c0 + 512, PW)
        pj[r0:r0 + rows, c0:c1] = _dot(h, win_ref[:, c0:c1])


def _mix_chunk(pj, r0, L, nv, LH, ssm, hst, prm, scr, mix_s, m0):
    cw, cb, dtb, alog, dskip, snw, hlb, hnw = prm
    y_s, o_s, b_s, kk_s, q_s, k_s, v_s = scr

    def conv_cols(c0, c1):
        acc = cb[:, c0:c1]
        for k in range(CONV_W):
            acc = acc + pj[r0 - 3 + k:r0 - 3 + k + L, X0 + c0:X0 + c1] * cw[k:k + 1, c0:c1]
        return _silu(acc)

    rowi = lax.broadcasted_iota(I32, (L, L), 0)
    coli = lax.broadcasted_iota(I32, (L, L), 1)
    causal = coli <= rowi
    rid = lax.broadcasted_iota(I32, (L, LANE), 0)
    lane = lax.broadcasted_iota(I32, (L, LANE), 1)
    lo = lane < SSD_P

    dt = _softplus(pj[r0:r0 + L, DT0:DT0 + LANE] + dtb[...])
    if nv < L:
        dt = jnp.where(rid < nv, dt, 0.0)
    a = dt * (-jnp.exp(alog[...]))
    acum = jnp.dot(causal.astype(F32), a, precision=HI, preferred_element_type=F32)
    acum_t = _dot_tn(a, (rowi <= coli).astype(F32), precision=HI)
    dt_t = _dot_tn(dt, (rowi == coli).astype(F32), precision=HI)
    eac = jnp.exp(acum)
    last = acum[L - 1:L, :]
    elast = jnp.exp(last)
    tail = jnp.exp(last - acum) * dt

    bm = [conv_cols(1024 + 128 * g, 1152 + 128 * g).astype(BF16) for g in range(2)]
    cm = [conv_cols(1280 + 128 * g, 1408 + 128 * g).astype(BF16) for g in range(2)]
    gcb = [_dot_nt(cm[g], bm[g]) for g in range(2)]
    rowh = lax.broadcasted_iota(I32, (LANE, LANE), 0) < SSD_P

    for j in range(SSD_HEADS // 2):
        g = j // 4
        cols = slice(128 * j, 128 * j + 128)
        xs = conv_cols(128 * j, 128 * j + 128)
        xsb = xs.astype(BF16)
        ys = []
        for h in (2 * j, 2 * j + 1):
            seg = acum[:, h:h + 1] - acum_t[h:h + 1, :]
            dec = jnp.exp(jnp.where(causal, seg, -jnp.inf))
            mh = (gcb[g] * dec * dt_t[h:h + 1, :]).astype(BF16)
            ys.append(_dot(mh, xsb))
        y = jnp.where(lo, ys[0], ys[1])
        hp = ssm[128 * j:128 * j + 128, :]
        yoff = _dot_nt(cm[g], hp.astype(BF16))
        y = y + yoff * jnp.where(lo, eac[:, 2 * j:2 * j + 1], eac[:, 2 * j + 1:2 * j + 2])
        y = y + dskip[:, cols] * xs
        y_s[0:L, cols] = y
        xt = (xs * jnp.where(lo, tail[:, 2 * j:2 * j + 1], tail[:, 2 * j + 1:2 * j + 2])).astype(BF16)
        upd = _dot_tn(xt, bm[g])
        el = jnp.where(rowh, elast[:, 2 * j:2 * j + 1], elast[:, 2 * j + 1:2 * j + 2])
        ssm[128 * j:128 * j + 128, :] = el * hp + upd

    yz = y_s[0:L, :] * _silu(pj[r0:r0 + L, Z0:Z0 + 1024])
    for g in range(2):
        sl = slice(512 * g, 512 * g + 512)
        part = yz[:, sl]
        ms = jnp.mean(part * part, axis=-1, keepdims=True)
        mix_s[m0:m0 + L, sl] = (part * lax.rsqrt(ms + EPS) * snw[:, sl]).astype(mix_s.dtype)

    r0v = hlb[0:1, :]
    r1v = hlb[1:2, :]
    mx = jnp.maximum(r0v, r1v)
    e0 = jnp.exp(r0v - mx)
    lb = e0 / (e0 + jnp.exp(r1v - mx))
    rid_w = lax.broadcasted_iota(I32, (L, 1024), 0)
    f = lb + (1.0 - lb) * jax.nn.sigmoid(pj[r0:r0 + L, F0:F0 + 1024])
    if nv < L:
        f = jnp.where(rid_w < nv, f, 1.0)
    kk = 1.0 - f
    blk = jnp.logical_and(causal, (rowi // LH) == (coli // LH)).astype(F32)
    b = jnp.dot(blk, jnp.log(f), precision=HI, preferred_element_type=F32)
    b_s[0:L, :] = b
    kk_s[0:L, :] = kk
    q_s[0:L, :] = (_silu(pj[r0:r0 + L, Q0:Q0 + 1024]) * jnp.exp(b)).astype(BF16)
    k_s[0:L, :] = (kk * jnp.exp(-b)).astype(BF16)
    v_s[0:L, :] = pj[r0:r0 + L, I0:I0 + 1024].astype(BF16)
    ri = lax.broadcasted_iota(I32, (LH, LH), 0)
    ci = lax.broadcasted_iota(I32, (LH, LH), 1)
    causal_h = ci <= ri

    def sub_chunk(i, carry):
        r = pl.multiple_of(i * LH, LH)
        bi = b_s[pl.ds(r, LH), :]
        bl = b_s[pl.ds(r + LH - 1, 1), :]
        kend = (kk_s[pl.ds(r, LH), :] * jnp.exp(bl - bi)).astype(BF16)
        ebl = jnp.exp(bl)
        qt = q_s[pl.ds(r, LH), :]
        kt = k_s[pl.ds(r, LH), :]
        v = v_s[pl.ds(r, LH), :]
        for hh in range(HGRN_HEADS):
            c = slice(128 * hh, 128 * hh + 128)
            att = jnp.where(causal_h, _dot_nt(qt[:, c], kt[:, c]), 0.0)
            st = hst[c, :]
            o = _dot(att.astype(BF16), v[:, c]) + _dot_nt(qt[:, c], st.astype(BF16))
            o_s[pl.ds(r, LH), c] = o
            hst[c, :] = ebl[:, c] * st + _dot_tn(v[:, c], kend[:, c])
        return carry

    lax.fori_loop(0, L // LH, sub_chunk, 0)

    for hh in range(HGRN_HEADS):
        c = slice(128 * hh, 128 * hh + 128)
        oh = o_s[0:L, c]
        ms = jnp.mean(oh * oh, axis=-1, keepdims=True)
        on = oh * lax.rsqrt(ms + EPS) * hnw[:, c]
        mix_s[m0:m0 + L, 1024 + 128 * hh:1152 + 128 * hh] = (
            on * _silu(pj[r0:r0 + L, G0 + 128 * hh:G0 + 128 * hh + 128])).astype(mix_s.dtype)


ROW_TILES = D_MODEL // LANE


def _store_rows(ref, val):
    for s in range(ROW_TILES):
        ref[:, s, :] = val[:, LANE * s:LANE * s + LANE]


def _out_part(mix, x, wout_ref, n2_ref, wrt_ref, br_ref):
    x1 = x + _dot(mix, wout_ref[...])
    ms = jnp.mean(x1 * x1, axis=-1, keepdims=True)
    xn2 = x1 * lax.rsqrt(ms + EPS) * n2_ref[...]
    lgt = _dot_nt(wrt_ref[...], xn2, precision=HI) + br_ref[...]
    return x1, xn2, lgt


def _prompt_kernel(tt, nt, x_ref, n1_ref, win_ref, cw, cb, dtb, alog, dskip, snw, hlb, hnw, wout_ref, n2_ref, wrt_ref,
                   br_ref, x1_ref, xn2_ref, lg_ref, conv_ref, ssm_ref, hgrn_ref,
                   pj, mix_s, hst, y_s, o_s, b_s, kk_s, q_s, k_s, v_s):
    t = pl.program_id(1)

    @pl.when(t == 0)
    def _():
        pj[0:HALO, :] = jnp.zeros((HALO, PW), F32)
        ssm_ref[0] = jnp.zeros((SSD_HEADS * SSD_P, SSD_N), F32)
        hst[...] = jnp.zeros_like(hst)

    x = x_ref[0]
    _in_proj(x, n1_ref, win_ref, pj, HALO)
    prm = (cw, cb, dtb, alog, dskip, snw, hlb, hnw)
    scr = (y_s, o_s, b_s, kk_s, q_s, k_s, v_s)
    for c in range(tt // SSD_CHUNK):
        _mix_chunk(pj, HALO + SSD_CHUNK * c, SSD_CHUNK, SSD_CHUNK, HGRN_CHUNK, ssm_ref.at[0], hst, prm, scr, mix_s,
                   SSD_CHUNK * c)
    x1, xn2, lgt = _out_part(mix_s[...], x, wout_ref, n2_ref, wrt_ref, br_ref)
    x1_ref[...] = x1
    _store_rows(xn2_ref, xn2)
    lg_ref[...] = lgt

    @pl.when(t == nt - 1)
    def _():
        conv_ref[0] = pj[HALO + tt - 3:HALO + tt, X0:X0 + CONV_DIM]
        for hh in range(HGRN_HEADS):
            c = slice(128 * hh, 128 * hh + 128)
            hgrn_ref[0, c, :] = hst[c, :].T

    pj[0:HALO, X0:X0 + CONV_DIM] = pj[tt:tt + HALO, X0:X0 + CONV_DIM]


def _const_spec(shape):
    nd = len(shape)
    return pl.BlockSpec(shape, lambda *_: (0,) * nd)


def _prompt_mixer(x, consts, tt=256):
    bsz, seq, _ = x.shape
    nt = seq // tt
    ta = bsz * seq
    n1, win, cw, cb, dtb, alog, dskip, snw, hlb, hnw, wout, n2, wrt, br = consts
    const_specs = [_const_spec(c.shape) for c in consts]
    out_shape = (
        jax.ShapeDtypeStruct((ta, D_MODEL), F32),
        jax.ShapeDtypeStruct((ta, ROW_TILES, LANE), F32),
        jax.ShapeDtypeStruct((N_EXPERTS, ta), F32),
        jax.ShapeDtypeStruct((bsz, CONV_W - 1, CONV_DIM), F32),
        jax.ShapeDtypeStruct((bsz, SSD_HEADS * SSD_P, SSD_N), F32),
        jax.ShapeDtypeStruct((bsz, HGRN_HEADS * HGRN_D, HGRN_D), F32),
    )
    out_specs = (
        pl.BlockSpec((tt, D_MODEL), lambda b, t: (b * nt + t, 0)),
        pl.BlockSpec((tt, ROW_TILES, LANE), lambda b, t: (b * nt + t, 0, 0)),
        pl.BlockSpec((N_EXPERTS, tt), lambda b, t: (0, b * nt + t)),
        pl.BlockSpec((1, CONV_W - 1, CONV_DIM), lambda b, t: (b, 0, 0)),
        pl.BlockSpec((1, SSD_HEADS * SSD_P, SSD_N), lambda b, t: (b, 0, 0)),
        pl.BlockSpec((1, HGRN_HEADS * HGRN_D, HGRN_D), lambda b, t: (b, 0, 0)),
    )
    wide = lambda dt: pltpu.VMEM((SSD_CHUNK, 1024), dt)
    scratch = [
        pltpu.VMEM((HALO + tt, PW), F32),
        pltpu.VMEM((tt, 2048), BF16),
        pltpu.VMEM((HGRN_HEADS * HGRN_D, HGRN_D), F32),
        wide(F32), wide(F32), wide(F32), wide(F32), wide(BF16), wide(BF16), wide(BF16),
    ]
    return pl.pallas_call(
        functools.partial(_prompt_kernel, tt, nt),
        grid=(bsz, nt),
        in_specs=[pl.BlockSpec((1, tt, D_MODEL), lambda b, t: (b, t, 0))] + const_specs,
        out_specs=out_specs,
        out_shape=out_shape,
        scratch_shapes=scratch,
        compiler_params=pltpu.CompilerParams(dimension_semantics=("arbitrary", "arbitrary"),
                                             vmem_limit_bytes=VMEM_LIMIT),
        name="prompt_mixer",
    )(x, *consts)


def _sample_inproj_kernel(x_ref, n1_ref, win_ref, o_ref):
    _in_proj(x_ref[...], n1_ref, win_ref, o_ref, 0)


def _sample_inproj(x, n1, win, rows=256):
    n = x.shape[0]
    return pl.pallas_call(
        _sample_inproj_kernel,
        grid=(n // rows,),
        in_specs=[pl.BlockSpec((rows, D_MODEL), lambda i: (i, 0)), _const_spec(n1.shape), _const_spec(win.shape)],
        out_specs=pl.BlockSpec((rows, PW), lambda i: (i, 0)),
        out_shape=jax.ShapeDtypeStruct((n, PW), F32),
        compiler_params=pltpu.CompilerParams(dimension_semantics=("arbitrary",), vmem_limit_bytes=VMEM_LIMIT),
        name="sample_inproj",
    )(x, n1, win)


SAMPLE_L = 16


def _sample_step_kernel(tdec, pj_ref, cin_ref, sin_ref, hin_ref, cw, cb, dtb, alog, dskip, snw, hlb, hnw,
                        mix_ref, cout_ref, sout_ref, hout_ref,
                        pj, mix_s, hst, y_s, o_s, b_s, kk_s, q_s, k_s, v_s):
    pj[...] = jnp.zeros_like(pj)
    pj[HALO - 3:HALO, X0:X0 + CONV_DIM] = cin_ref[0]
    pj[HALO:HALO + tdec, :] = pj_ref[0]
    sout_ref[0] = sin_ref[0]
    for hh in range(HGRN_HEADS):
        c = slice(128 * hh, 128 * hh + 128)
        hst[c, :] = hin_ref[0, c, :].T
    prm = (cw, cb, dtb, alog, dskip, snw, hlb, hnw)
    scr = (y_s, o_s, b_s, kk_s, q_s, k_s, v_s)
    _mix_chunk(pj, HALO, SAMPLE_L, tdec, SAMPLE_L, sout_ref.at[0], hst, prm, scr, mix_s, 0)
    mix_ref[0] = mix_s[0:tdec, :]
    cout_ref[0] = pj[HALO + tdec - 3:HALO + tdec, X0:X0 + CONV_DIM]
    for hh in range(HGRN_HEADS):
        c = slice(128 * hh, 128 * hh + 128)
        hout_ref[0, c, :] = hst[c, :].T


def _sample_step(proj, conv_in, ssm_in, hgrn_in, consts):
    nseq, tdec, _ = proj.shape
    seq_spec = lambda shape: pl.BlockSpec((1,) + shape, lambda i: (i,) + (0,) * len(shape))
    wide = lambda dt: pltpu.VMEM((SAMPLE_L, 1024), dt)
    scratch = [
        pltpu.VMEM((HALO + SAMPLE_L, PW), F32),
        pltpu.VMEM((SAMPLE_L, 2048), F32),
        pltpu.VMEM((HGRN_HEADS * HGRN_D, HGRN_D), F32),
        wide(F32), wide(F32), wide(F32), wide(F32), wide(BF16), wide(BF16), wide(BF16),
    ]
    return pl.pallas_call(
        functools.partial(_sample_step_kernel, tdec),
        grid=(nseq,),
        in_specs=[seq_spec((tdec, PW)), seq_spec((CONV_W - 1, CONV_DIM)), seq_spec((SSD_HEADS * SSD_P, SSD_N)),
                  seq_spec((HGRN_HEADS * HGRN_D, HGRN_D))] + [_const_spec(c.shape) for c in consts],
        out_specs=(seq_spec((tdec, 2048)), seq_spec((CONV_W - 1, CONV_DIM)), seq_spec((SSD_HEADS * SSD_P, SSD_N)),
                   seq_spec((HGRN_HEADS * HGRN_D, HGRN_D))),
        out_shape=(jax.ShapeDtypeStruct((nseq, tdec, 2048), F32),
                   jax.ShapeDtypeStruct((nseq, CONV_W - 1, CONV_DIM), F32),
                   jax.ShapeDtypeStruct((nseq, SSD_HEADS * SSD_P, SSD_N), F32),
                   jax.ShapeDtypeStruct((nseq, HGRN_HEADS * HGRN_D, HGRN_D), F32)),
        scratch_shapes=scratch,
        compiler_params=pltpu.CompilerParams(dimension_semantics=("arbitrary",), vmem_limit_bytes=VMEM_LIMIT),
        name="sample_step",
    )(proj, conv_in, ssm_in, hgrn_in, *consts)


def _sample_out_kernel(mix_ref, x_ref, wout_ref, n2_ref, wrt_ref, br_ref, x1_ref, xn2_ref, lg_ref):
    x1, xn2, lgt = _out_part(mix_ref[...].astype(BF16), x_ref[...], wout_ref, n2_ref, wrt_ref, br_ref)
    x1_ref[...] = x1
    _store_rows(xn2_ref, xn2)
    lg_ref[...] = lgt


def _sample_out(mix, x, wout, n2, wrt, br):
    n = x.shape[0]
    return pl.pallas_call(
        _sample_out_kernel,
        out_shape=(jax.ShapeDtypeStruct((n, D_MODEL), F32), jax.ShapeDtypeStruct((n, ROW_TILES, LANE), F32),
                   jax.ShapeDtypeStruct((N_EXPERTS, n), F32)),
        compiler_params=pltpu.CompilerParams(vmem_limit_bytes=VMEM_LIMIT),
        name="sample_out",
    )(mix, x, wout, n2, wrt, br)


ROUTE_TILE = 256


def _route_kernel(n_first, lga_ref, lgb_ref, g_ref, d_ref, cnt_ref, cnt, carry, pstart):
    ph = pl.program_id(0)
    i = pl.program_id(1)
    tl = ROUTE_TILE
    l = jnp.where(i < n_first, lga_ref[...], lgb_ref[...])
    eid = lax.broadcasted_iota(I32, (N_EXPERTS, tl), 0)
    hots, vals = [], []
    for _ in range(TOP_K):
        m = jnp.max(l, axis=0, keepdims=True)
        idx = jnp.min(jnp.where(l == m, eid, N_EXPERTS), axis=0, keepdims=True)
        hot = eid == idx
        hots.append(hot)
        vals.append(m)
        l = jnp.where(hot, -jnp.inf, l)
    ind = hots[0].astype(F32)
    for hot in hots[1:]:
        ind = ind + hot.astype(F32)
    tile_cnt = jnp.sum(ind, axis=1, keepdims=True)

    @pl.when(jnp.logical_and(ph == 0, i == 0))
    def _():
        cnt[...] = jnp.zeros_like(cnt)

    @pl.when(ph == 0)
    def _():
        cnt[...] = cnt[...] + tile_cnt

    @pl.when(jnp.logical_and(ph == 1, i == 0))
    def _():
        padded = jnp.floor((cnt[...] + (MOE_BLOCK - 1)) * (1.0 / MOE_BLOCK)) * MOE_BLOCK
        r = lax.broadcasted_iota(I32, (N_EXPERTS, N_EXPERTS), 0)
        c = lax.broadcasted_iota(I32, (N_EXPERTS, N_EXPERTS), 1)
        pend = jnp.dot((c <= r).astype(F32), padded, precision=HI, preferred_element_type=F32)
        pstart[...] = pend - padded
        carry[...] = jnp.zeros_like(carry)
        cnt_ref[...] = cnt[...]

    @pl.when(ph == 1)
    def _():
        rr = lax.broadcasted_iota(I32, (tl, tl), 0)
        cc = lax.broadcasted_iota(I32, (tl, tl), 1)
        before = _dot(ind.astype(BF16), (rr < cc).astype(BF16))
        base = before + carry[:, 0:1] + pstart[:, 0:1]
        den = jnp.exp(vals[0] - vals[0])
        for k in range(1, TOP_K):
            den = den + jnp.exp(vals[k] - vals[0])
        g_ref[...] = jnp.zeros_like(g_ref)
        for k in range(TOP_K):
            g_ref[k:k + 1, :] = jnp.exp(vals[k] - vals[0]) / den
            d_ref[k:k + 1, :] = jnp.sum(jnp.where(hots[k], base, 0.0), axis=0, keepdims=True).astype(I32)
        carry[...] = carry[...] + tile_cnt


def _two_source_maps(n_first, trailing=1):
    pad = (0,) * trailing
    return (lambda *idx: (jnp.minimum(idx[-1], n_first - 1),) + pad,
            lambda *idx: (jnp.maximum(idx[-1] - n_first, 0),) + pad)


def _route(lg_a, lg_b):
    ta = lg_a.shape[1] + lg_b.shape[1]
    nt = ta // ROUTE_TILE
    n_first = lg_a.shape[1] // ROUTE_TILE
    assert n_first * ROUTE_TILE == lg_a.shape[1] and nt * ROUTE_TILE == ta
    return pl.pallas_call(
        functools.partial(_route_kernel, n_first),
        grid=(2, nt),
        in_specs=[pl.BlockSpec((N_EXPERTS, ROUTE_TILE), lambda p, i: (0, jnp.minimum(i, n_first - 1))),
                  pl.BlockSpec((N_EXPERTS, ROUTE_TILE), lambda p, i: (0, jnp.maximum(i - n_first, 0)))],
        out_specs=(pl.BlockSpec((8, ROUTE_TILE), lambda p, i: (0, i * p)),
                   pl.BlockSpec((None, TOP_K, ROUTE_TILE), lambda p, i: (i * p, 0, 0)),
                   pl.BlockSpec((N_EXPERTS, LANE), lambda p, i: (0, 0))),
        out_shape=(jax.ShapeDtypeStruct((8, ta), F32), jax.ShapeDtypeStruct((nt, TOP_K, ROUTE_TILE), I32),
                   jax.ShapeDtypeStruct((N_EXPERTS, LANE), F32)),
        scratch_shapes=[pltpu.VMEM((N_EXPERTS, LANE), F32)] * 3,
        compiler_params=pltpu.CompilerParams(dimension_semantics=("arbitrary", "arbitrary")),
        name="route",
    )(lg_a, lg_b)


DMA_UNROLL = 2


def _row_copy(src, dst, sem):
    return pltpu.make_async_copy(src, dst, sem)


def _dispatch_kernel(n_first, pend_ref, padded_ref, nu_ref, d_ref, xa_ref, xb_ref, o_ref, zeros, sem):
    i = pl.program_id(0)
    tl = ROUTE_TILE
    nb = o_ref.shape[0] // MOE_BLOCK

    @pl.when(i == 0)
    def _():
        zeros[...] = jnp.zeros_like(zeros)

        def fills(e):
            return ((padded_ref[e] > 0, pend_ref[e] - MOE_BLOCK),
                    (nu_ref[0] + e < nb, (nu_ref[0] + e) * MOE_BLOCK))

        for e in range(N_EXPERTS):
            for cond, row in fills(e):
                @pl.when(cond)
                def _(row=row):
                    _row_copy(zeros, o_ref.at[pl.ds(row, MOE_BLOCK)], sem).start()
        for e in range(N_EXPERTS):
            for cond, row in fills(e):
                @pl.when(cond)
                def _(row=row):
                    _row_copy(zeros, o_ref.at[pl.ds(row, MOE_BLOCK)], sem).wait()

    def scatter_rows(x_ref):
        def start(j2, c):
            for u in range(DMA_UNROLL):
                j = j2 * DMA_UNROLL + u
                for k in range(TOP_K):
                    _row_copy(x_ref.at[j], o_ref.at[d_ref[k, j]], sem).start(priority=(u + k) % 2)
            return c

        lax.fori_loop(0, tl // DMA_UNROLL, start, 0)

        def wait(j2, c):
            for u in range(DMA_UNROLL):
                j = j2 * DMA_UNROLL + u
                for k in range(TOP_K):
                    _row_copy(x_ref.at[j], o_ref.at[d_ref[k, j]], sem).wait()
            return c

        lax.fori_loop(0, tl // DMA_UNROLL, wait, 0)

    @pl.when(i < n_first)
    def _():
        scatter_rows(xa_ref)

    @pl.when(i >= n_first)
    def _():
        scatter_rows(xb_ref)


def _dispatch(xa, xb, d3, pend, padded, n_used, n_rows):
    nt = (xa.shape[0] + xb.shape[0]) // ROUTE_TILE
    n_first = xa.shape[0] // ROUTE_TILE
    map_a, map_b = _two_source_maps(n_first, trailing=2)
    grid_spec = pltpu.PrefetchScalarGridSpec(
        num_scalar_prefetch=3,
        grid=(nt,),
        in_specs=[pl.BlockSpec((None, TOP_K, ROUTE_TILE), lambda i, *_: (i, 0, 0), memory_space=pltpu.SMEM),
                  pl.BlockSpec((ROUTE_TILE, ROW_TILES, LANE), lambda i, *_: map_a(i)),
                  pl.BlockSpec((ROUTE_TILE, ROW_TILES, LANE), lambda i, *_: map_b(i))],
        out_specs=pl.BlockSpec(memory_space=pl.ANY),
        scratch_shapes=[pltpu.VMEM((MOE_BLOCK, ROW_TILES, LANE), F32), pltpu.SemaphoreType.DMA],
    )
    return pl.pallas_call(
        functools.partial(_dispatch_kernel, n_first),
        grid_spec=grid_spec,
        out_shape=jax.ShapeDtypeStruct((n_rows, ROW_TILES, LANE), F32),
        compiler_params=pltpu.CompilerParams(dimension_semantics=("arbitrary",)),
        name="dispatch",
    )(pend, padded, n_used, d3, xa, xb)


def _expert_kernel(be_ref, nu_ref, x_ref, wg_ref, bg_ref, wu_ref, bu_ref, wd_ref, bd_ref, o_ref, wg_s, wu_s, wd_s,
                   lhs_s):
    i = pl.program_id(0)
    prev = be_ref[jnp.maximum(i - 1, 0)]
    fresh = jnp.logical_or(i == 0, be_ref[i] != prev)

    @pl.when(jnp.logical_and(i < nu_ref[0], fresh))
    def _():
        wg_s[...] = wg_ref[0].astype(BF16)
        wu_s[...] = wu_ref[0].astype(BF16)
        wd_s[...] = wd_ref[0].astype(BF16)

    @pl.when(i < nu_ref[0])
    def _():
        for s in range(ROW_TILES):
            lhs_s[:, LANE * s:LANE * s + LANE] = x_ref[:, s, :].astype(BF16)
        x = lhs_s[...]
        g = _dot(x, wg_s[...]) + bg_ref[0]
        u = _dot(x, wu_s[...]) + bu_ref[0]
        g = jnp.minimum(g, SWIGLU_LIMIT)
        u = jnp.clip(u, -SWIGLU_LIMIT, SWIGLU_LIMIT)
        act = g * jax.nn.sigmoid(SWIGLU_ALPHA * g) * (u + 1.0)
        _store_rows(o_ref, _dot(act.astype(BF16), wd_s[...]) + bd_ref[0])

    @pl.when(i >= nu_ref[0])
    def _():
        o_ref[...] = jnp.zeros_like(o_ref)


def _experts(xs, block_e, n_used, wg, bg, wu, bu, wd, bd):
    n_rows = xs.shape[0]
    nb = n_rows // MOE_BLOCK
    row_map = lambda i, be, nu: (jnp.minimum(i, nu[0] - 1), 0, 0)
    w_map = lambda i, be, nu: (be[i], 0, 0)
    w_spec = pl.BlockSpec((1, D_MODEL, D_MODEL), w_map)
    b_spec = pl.BlockSpec((1, 1, D_MODEL), w_map)
    grid_spec = pltpu.PrefetchScalarGridSpec(
        num_scalar_prefetch=2,
        grid=(nb,),
        in_specs=[pl.BlockSpec((MOE_BLOCK, ROW_TILES, LANE), row_map), w_spec, b_spec, w_spec, b_spec, w_spec,
                  b_spec],
        out_specs=pl.BlockSpec((MOE_BLOCK, ROW_TILES, LANE), lambda i, be, nu: (i, 0, 0)),
        scratch_shapes=[pltpu.VMEM((D_MODEL, D_MODEL), BF16)] * 3 + [pltpu.VMEM((MOE_BLOCK, D_MODEL), BF16)],
    )
    return pl.pallas_call(
        _expert_kernel,
        grid_spec=grid_spec,
        out_shape=jax.ShapeDtypeStruct((n_rows, ROW_TILES, LANE), F32),
        compiler_params=pltpu.CompilerParams(dimension_semantics=("arbitrary",), vmem_limit_bytes=VMEM_LIMIT),
        name="experts",
    )(block_e, n_used, xs, wg, bg.reshape(N_EXPERTS, 1, D_MODEL), wu, bu.reshape(N_EXPERTS, 1, D_MODEL), wd,
      bd.reshape(N_EXPERTS, 1, D_MODEL))


def _combine_kernel(n_first, d_ref, x1a_ref, x1b_ref, g_ref, nf_ref, eo_ref, ya_ref, yb_ref, buf, y_s, sem):
    i = pl.program_id(0)
    tl = ROUTE_TILE

    def start(j2, c):
        for u in range(DMA_UNROLL):
            j = j2 * DMA_UNROLL + u
            for k in range(TOP_K):
                _row_copy(eo_ref.at[d_ref[k, j]], buf.at[k, j], sem).start(priority=(u + k) % 2)
        return c

    lax.fori_loop(0, tl // DMA_UNROLL, start, 0)
    sel = (lax.broadcasted_iota(I32, (8, LANE), 0) == lax.broadcasted_iota(I32, (8, LANE), 1)).astype(F32)
    gt = _dot_tn(g_ref[...], sel, precision=HI)

    def wait(j2, c):
        for u in range(DMA_UNROLL):
            j = j2 * DMA_UNROLL + u
            for k in range(TOP_K):
                _row_copy(eo_ref.at[d_ref[k, j]], buf.at[k, j], sem).wait()
        return c

    lax.fori_loop(0, tl // DMA_UNROLL, wait, 0)
    ss = jnp.zeros((tl, 1), F32)
    for s in range(8):
        cols = slice(LANE * s, LANE * s + LANE)
        moe = gt[:, 0:1] * buf[0, :, s, :]
        for k in range(1, TOP_K):
            moe = moe + gt[:, k:k + 1] * buf[k, :, s, :]
        y = jnp.where(i < n_first, x1a_ref[:, cols], x1b_ref[:, cols]) + moe
        ss = ss + jnp.sum(y * y, axis=-1, keepdims=True)
        y_s[:, cols] = y
    y = y_s[...] * lax.rsqrt(ss * (1.0 / D_MODEL) + EPS) * nf_ref[...]

    @pl.when(i < n_first)
    def _():
        ya_ref[...] = y

    @pl.when(i >= n_first)
    def _():
        yb_ref[...] = y


def _combine(x1a, x1b, gates8, d3, eo3, nf):
    nt = (x1a.shape[0] + x1b.shape[0]) // ROUTE_TILE
    n_first = x1a.shape[0] // ROUTE_TILE
    map_a, map_b = _two_source_maps(n_first)
    row_specs = (pl.BlockSpec((ROUTE_TILE, D_MODEL), map_a), pl.BlockSpec((ROUTE_TILE, D_MODEL), map_b))
    grid_spec = pltpu.PrefetchScalarGridSpec(
        num_scalar_prefetch=0,
        grid=(nt,),
        in_specs=[pl.BlockSpec((None, TOP_K, ROUTE_TILE), lambda i: (i, 0, 0), memory_space=pltpu.SMEM),
                  *row_specs,
                  pl.BlockSpec((8, ROUTE_TILE), lambda i: (0, i)),
                  _const_spec(nf.shape),
                  pl.BlockSpec(memory_space=pl.ANY)],
        out_specs=row_specs,
        scratch_shapes=[pltpu.VMEM((TOP_K, ROUTE_TILE, ROW_TILES, LANE), F32),
                        pltpu.VMEM((ROUTE_TILE, D_MODEL), F32), pltpu.SemaphoreType.DMA],
    )
    return pl.pallas_call(
        functools.partial(_combine_kernel, n_first),
        grid_spec=grid_spec,
        out_shape=(jax.ShapeDtypeStruct(x1a.shape, F32), jax.ShapeDtypeStruct(x1b.shape, F32)),
        compiler_params=pltpu.CompilerParams(dimension_semantics=("arbitrary",), vmem_limit_bytes=VMEM_LIMIT),
        name="combine",
    )(d3, x1a, x1b, gates8, nf, eo3)


def _moe_and_norm(x1, xn2, lgt, wg, bg, wu, bu, wd, bd, nf):
    ta = x1[0].shape[0] + x1[1].shape[0]
    n_rows = (-(-(ta * TOP_K) // MOE_BLOCK) + N_EXPERTS) * MOE_BLOCK
    gates8, d3, cnt = _route(*lgt)
    counts = cnt[:, 0].astype(I32)
    padded = (counts + MOE_BLOCK - 1) // MOE_BLOCK * MOE_BLOCK
    pend = jnp.cumsum(padded)
    n_used = (pend[-1:] // MOE_BLOCK).astype(I32)
    block_row = jnp.arange(n_rows // MOE_BLOCK, dtype=I32) * MOE_BLOCK
    block_e = jnp.minimum(jnp.sum((pend[None, :] <= block_row[:, None]).astype(I32), axis=1), N_EXPERTS - 1)
    xs = _dispatch(*xn2, d3, pend.astype(I32), padded.astype(I32), n_used, n_rows)
    eo = _experts(xs, block_e, n_used, wg, bg, wu, bu, wd, bd)
    return _combine(*x1, gates8, d3, eo, nf)


def _prep_consts(norm1_w, w_in, conv_w, conv_b, dt_bias, A_log, D_skip, ssd_norm_w, hgrn_lower_bound, hgrn_norm_w,
                 w_out, norm2_w, w_router, b_router):
    row = lambda v: v.reshape(1, -1).astype(F32)
    pad_lane = lambda v: jnp.pad(v.reshape(1, -1).astype(F32), ((0, 0), (0, LANE - v.size)))
    win = jnp.concatenate([w_in[:, :2560], w_in[:, 2576:6672], w_in[:, 2560:2576],
                           jnp.zeros((D_MODEL, LANE - SSD_HEADS), w_in.dtype)], axis=1).astype(BF16)
    dskip = jnp.repeat(D_skip.astype(F32), SSD_P).reshape(1, -1)
    return (row(norm1_w), win, conv_w.astype(F32), row(conv_b), pad_lane(dt_bias), pad_lane(A_log), dskip,
            row(ssd_norm_w), hgrn_lower_bound.astype(F32), row(hgrn_norm_w), w_out.astype(BF16), row(norm2_w),
            w_router.T.astype(F32), b_router.reshape(-1, 1).astype(F32))


def kernel(x_prompt, x_sample, state_conv, state_ssm, state_hgrn, norm1_w, w_in, conv_w, conv_b, dt_bias, A_log, D_skip, ssd_norm_w, hgrn_lower_bound, hgrn_norm_w, w_out, norm2_w, w_router, b_router, w_gate, b_gate, w_up, b_up, w_down, b_down, norm_f_w):
    depth = w_in.shape[0]
    assert depth == 1
    bsz, seq, _ = x_prompt.shape
    nseq, tdec, _ = x_sample.shape
    consts = _prep_consts(norm1_w[0], w_in[0], conv_w[0], conv_b[0], dt_bias[0], A_log[0], D_skip[0], ssd_norm_w[0],
                          hgrn_lower_bound, hgrn_norm_w[0], w_out[0], norm2_w[0], w_router[0], b_router[0])
    n1, win, cw, cb, dtb, alog, dskip, snw, hlb, hnw, wout, n2, wrt, br = consts
    chunk_consts = (cw, cb, dtb, alog, dskip, snw, hlb, hnw)

    x1_p, xn2_p, lg_p, conv_p, ssm_p, hgrn_p = _prompt_mixer(x_prompt, consts)

    xs_flat = x_sample.reshape(nseq * tdec, D_MODEL)
    proj_s = _sample_inproj(xs_flat, n1, win).reshape(nseq, tdec, PW)
    mix_s, conv_s, ssm_s, hgrn_s = _sample_step(
        proj_s, state_conv[0], state_ssm[0].reshape(nseq, SSD_HEADS * SSD_P, SSD_N),
        state_hgrn[0].reshape(nseq, HGRN_HEADS * HGRN_D, HGRN_D), chunk_consts)
    x1_s, xn2_s, lg_s = _sample_out(mix_s.reshape(nseq * tdec, 2048), xs_flat, wout, n2, wrt, br)

    y_p, y_s = _moe_and_norm((x1_p, x1_s), (xn2_p, xn2_s), (lg_p, lg_s), w_gate[0], b_gate[0], w_up[0], b_up[0],
                             w_down[0], b_down[0], norm_f_w.reshape(1, -1).astype(F32))
    y_prompt = y_p.reshape(bsz, seq, D_MODEL)
    y_sample = y_s.reshape(nseq, tdec, D_MODEL)
    return (y_prompt, y_sample,
            conv_p[None], ssm_p.reshape(1, bsz, SSD_HEADS, SSD_P, SSD_N),
            hgrn_p.reshape(1, bsz, HGRN_HEADS, HGRN_D, HGRN_D),
            conv_s[None], ssm_s.reshape(1, nseq, SSD_HEADS, SSD_P, SSD_N),
            hgrn_s.reshape(1, nseq, HGRN_HEADS, HGRN_D, HGRN_D))
```

```python
import functools

import jax
import jax.numpy as jnp
from jax import lax
from jax.experimental import pallas as pl
from jax.experimental.pallas import tpu as pltpu

F32 = jnp.float32
BF16 = jnp.bfloat16
I32 = jnp.int32
HI = lax.Precision.HIGHEST

D_MODEL = 1024
SSD_HEADS = 16
SSD_P = 64
SSD_N = 128
CONV_DIM = 1536
CONV_W = 4
HGRN_HEADS = 8
HGRN_D = 128
N_EXPERTS = 32
TOP_K = 4
MOE_BLOCK = 256
SWIGLU_LIMIT = 7.0
SWIGLU_ALPHA = 1.702
EPS = 1e-6

Z0 = 0
X0 = 1024
Q0 = 2560
F0 = 3584
I0 = 4608
G0 = 5632
DT0 = 6656
PW = 6784
LANE = 128
HALO = 8
SSD_CHUNK = 128
HGRN_CHUNK = 16
MAX_SUB = SSD_CHUNK // HGRN_CHUNK
VMEM_LIMIT = 56 * 1024 * 1024


def _silu(x):
    return x * jax.nn.sigmoid(x)


def _softplus(x):
    return jnp.maximum(x, 0.0) + jnp.log1p(jnp.exp(-jnp.abs(x)))


def _dot(a, b):
    return jnp.dot(a, b, preferred_element_type=F32)


def _dot_nt(a, b, precision=None):
    return lax.dot_general(a, b, (((1,), (1,)), ((), ())), precision=precision, preferred_element_type=F32)


def _dot_tn(a, b, precision=None):
    return lax.dot_general(a, b, (((0,), (0,)), ((), ())), precision=precision, preferred_element_type=F32)


def _in_proj(x, n1_ref, win_ref, pj, r0):
    rows = x.shape[0]
    ms = jnp.mean(x * x, axis=-1, keepdims=True)
    h = (x * lax.rsqrt(ms + EPS) * n1_ref[...]).astype(BF16)
    for c0 in range(0, PW, 512):
        c1 = min(c0 + 512, PW)
        pj[r0:r0 + rows, c0:c1] = _dot(h, win_ref[:, c0:c1])


def _mix_chunk(pj, r0, L, nv, LH, ssm, hst, prm, scr, mix_s, m0):
    cw, cb, dtb, alog, dskip, snw, hlb, hnw = prm
    y_s, b_s, kk_s, r_s, q_s, qb_s, v_s, kend_s = scr

    def conv_cols(c0, c1):
        acc = cb[:, c0:c1]
        for k in range(CONV_W):
            acc = acc + pj[r0 - 3 + k:r0 - 3 + k + L, X0 + c0:X0 + c1] * cw[k:k + 1, c0:c1]
        return _silu(acc)

    rowi = lax.broadcasted_iota(I32, (L, L), 0)
    coli = lax.broadcasted_iota(I32, (L, L), 1)
    causal = coli <= rowi
    rid = lax.broadcasted_iota(I32, (L, LANE), 0)
    lane = lax.broadcasted_iota(I32, (L, LANE), 1)
    lo = lane < SSD_P

    dt = _softplus(pj[r0:r0 + L, DT0:DT0 + LANE] + dtb[...])
    if nv < L:
        dt = jnp.where(rid < nv, dt, 0.0)
    a = dt * (-jnp.exp(alog[...]))
    acum = jnp.dot(causal.astype(F32), a, precision=HI, preferred_element_type=F32)
    acum_t = _dot_tn(a, (rowi <= coli).astype(F32), precision=HI)
    dt_t = _dot_tn(dt, (rowi == coli).astype(F32), precision=HI)
    eac = jnp.exp(acum)
    last = acum[L - 1:L, :]
    elast = jnp.exp(last)
    tail = jnp.exp(last - acum) * dt

    bm = [conv_cols(1024 + 128 * g, 1152 + 128 * g).astype(BF16) for g in range(2)]
    cm = [conv_cols(1280 + 128 * g, 1408 + 128 * g).astype(BF16) for g in range(2)]
    gcb = [_dot_nt(cm[g], bm[g]) for g in range(2)]
    rowh = lax.broadcasted_iota(I32, (LANE, LANE), 0) < SSD_P

    for j in range(SSD_HEADS // 2):
        g = j // 4
        cols = slice(128 * j, 128 * j + 128)
        xs = conv_cols(128 * j, 128 * j + 128)
        xsb = xs.astype(BF16)
        ys = []
        for h in (2 * j, 2 * j + 1):
            seg = acum[:, h:h + 1] - acum_t[h:h + 1, :]
            dec = jnp.exp(jnp.where(causal, seg, -jnp.inf))
            mh = (gcb[g] * dec * dt_t[h:h + 1, :]).astype(BF16)
            ys.append(_dot(mh, xsb))
        y = jnp.where(lo, ys[0], ys[1])
        hp = ssm[128 * j:128 * j + 128, :]
        yoff = _dot_nt(cm[g], hp.astype(BF16))
        y = y + yoff * jnp.where(lo, eac[:, 2 * j:2 * j + 1], eac[:, 2 * j + 1:2 * j + 2])
        y = y + dskip[:, cols] * xs
        y_s[0:L, cols] = y
        xt = (xs * jnp.where(lo, tail[:, 2 * j:2 * j + 1], tail[:, 2 * j + 1:2 * j + 2])).astype(BF16)
        upd = _dot_tn(xt, bm[g])
        el = jnp.where(rowh, elast[:, 2 * j:2 * j + 1], elast[:, 2 * j + 1:2 * j + 2])
        ssm[128 * j:128 * j + 128, :] = el * hp + upd

    yz = y_s[0:L, :] * _silu(pj[r0:r0 + L, Z0:Z0 + 1024])
    for g in range(2):
        sl = slice(512 * g, 512 * g + 512)
        part = yz[:, sl]
        ms = jnp.mean(part * part, axis=-1, keepdims=True)
        mix_s[m0:m0 + L, sl] = (part * lax.rsqrt(ms + EPS) * snw[:, sl]).astype(mix_s.dtype)

    n_sub = L // LH
    r0v = hlb[0:1, :]
    r1v = hlb[1:2, :]
    mx = jnp.maximum(r0v, r1v)
    e0 = jnp.exp(r0v - mx)
    lb = e0 / (e0 + jnp.exp(r1v - mx))
    rid_w = lax.broadcasted_iota(I32, (L, 1024), 0)
    f = lb + (1.0 - lb) * jax.nn.sigmoid(pj[r0:r0 + L, F0:F0 + 1024])
    if nv < L:
        f = jnp.where(rid_w < nv, f, 1.0)
    kk = 1.0 - f
    blk = jnp.logical_and(causal, (rowi // LH) == (coli // LH)).astype(F32)
    b = jnp.dot(blk, jnp.log(f), precision=HI, preferred_element_type=F32)
    q = _silu(pj[r0:r0 + L, Q0:Q0 + 1024])
    base = jnp.zeros((1, 1024), F32)
    for i in range(n_sub):
        rows = slice(LH * i, LH * i + LH)
        bi = b[rows, :]
        q_s[rows, :] = (q[rows, :] * jnp.exp(bi)).astype(BF16)
        qb_s[rows, :] = (q[rows, :] * jnp.exp(bi + base)).astype(BF16)
        b_s[rows, :] = bi + base
        r_s[i:i + 1, :] = base
        base = base + bi[LH - 1:LH, :]
    kk_s[0:L, :] = kk
    v_s[0:L, :] = pj[r0:r0 + L, I0:I0 + 1024].astype(BF16)
    kend_s[0:L, :] = (kk * jnp.exp(base - b_s[0:L, :])).astype(BF16)
    ebl = jnp.exp(base)
    colq = lax.broadcasted_iota(I32, (LH, L), 1)
    rowq = lax.broadcasted_iota(I32, (LH, L), 0)

    for hh in range(HGRN_HEADS):
        c = slice(128 * hh, 128 * hh + 128)
        bh = b_s[0:L, c]
        kkh = kk_s[0:L, c]
        parts = []
        for i in range(n_sub):
            expo = jnp.where(rid <= LH * i + LH - 1, r_s[i:i + 1, c] - bh, -jnp.inf)
            ktil = (kkh * jnp.exp(expo)).astype(BF16)
            att = _dot_nt(q_s[LH * i:LH * i + LH, c], ktil)
            parts.append(jnp.where(colq <= rowq + LH * i, att, 0.0).astype(BF16))
        att_all = parts[0] if n_sub == 1 else jnp.concatenate(parts, axis=0)
        st = hst[c, :]
        oh = _dot(att_all, v_s[0:L, c]) + _dot_nt(qb_s[0:L, c], st.astype(BF16))
        hst[c, :] = ebl[:, c] * st + _dot_tn(v_s[0:L, c], kend_s[0:L, c])
        ms = jnp.mean(oh * oh, axis=-1, keepdims=True)
        on = oh * lax.rsqrt(ms + EPS) * hnw[:, c]
        mix_s[m0:m0 + L, 1024 + 128 * hh:1152 + 128 * hh] = (
            on * _silu(pj[r0:r0 + L, G0 + 128 * hh:G0 + 128 * hh + 128])).astype(mix_s.dtype)


def _chunk_scratch(rows):
    wide = lambda dt: pltpu.VMEM((rows, 1024), dt)
    return [wide(F32), wide(F32), wide(F32), pltpu.VMEM((MAX_SUB, 1024), F32), wide(BF16), wide(BF16), wide(BF16),
            wide(BF16)]


ROW_TILES = D_MODEL // LANE


def _store_rows(ref, val):
    for s in range(ROW_TILES):
        ref[:, s, :] = val[:, LANE * s:LANE * s + LANE]


def _load_rows(ref):
    planes = jnp.swapaxes(ref[...], 0, 1)
    return [planes[s] for s in range(ROW_TILES)]


def _out_part(mix, x, wout_ref, n2_ref, wrt_ref, br_ref):
    x1 = x + _dot(mix, wout_ref[...])
    ms = jnp.mean(x1 * x1, axis=-1, keepdims=True)
    xn2 = x1 * lax.rsqrt(ms + EPS) * n2_ref[...]
    lgt = _dot_nt(wrt_ref[...], xn2, precision=HI) + br_ref[...]
    return x1, xn2, lgt


def _prompt_kernel(tt, nt, x_ref, n1_ref, win_ref, cw, cb, dtb, alog, dskip, snw, hlb, hnw, wout_ref, n2_ref, wrt_ref,
                   br_ref, x1_ref, xn2_ref, lg_ref, conv_ref, ssm_ref, hgrn_ref,
                   pj, mix_s, hst, *scr):
    t = pl.program_id(1)

    @pl.when(t == 0)
    def _():
        pj[0:HALO, :] = jnp.zeros((HALO, PW), F32)
        ssm_ref[0] = jnp.zeros((SSD_HEADS * SSD_P, SSD_N), F32)
        hst[...] = jnp.zeros_like(hst)

    x = x_ref[0]
    _in_proj(x, n1_ref, win_ref, pj, HALO)
    prm = (cw, cb, dtb, alog, dskip, snw, hlb, hnw)
    for c in range(tt // SSD_CHUNK):
        _mix_chunk(pj, HALO + SSD_CHUNK * c, SSD_CHUNK, SSD_CHUNK, HGRN_CHUNK, ssm_ref.at[0], hst, prm, scr, mix_s,
                   SSD_CHUNK * c)
    x1, xn2, lgt = _out_part(mix_s[...], x, wout_ref, n2_ref, wrt_ref, br_ref)
    x1_ref[...] = x1
    _store_rows(xn2_ref, xn2)
    lg_ref[...] = lgt

    @pl.when(t == nt - 1)
    def _():
        conv_ref[0] = pj[HALO + tt - 3:HALO + tt, X0:X0 + CONV_DIM]
        for hh in range(HGRN_HEADS):
            c = slice(128 * hh, 128 * hh + 128)
            hgrn_ref[0, c, :] = hst[c, :].T

    pj[0:HALO, X0:X0 + CONV_DIM] = pj[tt:tt + HALO, X0:X0 + CONV_DIM]


def _const_spec(shape):
    nd = len(shape)
    return pl.BlockSpec(shape, lambda *_: (0,) * nd)


def _prompt_mixer(x, consts, tt=256):
    bsz, seq, _ = x.shape
    nt = seq // tt
    ta = bsz * seq
    const_specs = [_const_spec(c.shape) for c in consts]
    out_shape = (
        jax.ShapeDtypeStruct((ta, D_MODEL), F32),
        jax.ShapeDtypeStruct((ta, ROW_TILES, LANE), F32),
        jax.ShapeDtypeStruct((N_EXPERTS, ta), F32),
        jax.ShapeDtypeStruct((bsz, CONV_W - 1, CONV_DIM), F32),
        jax.ShapeDtypeStruct((bsz, SSD_HEADS * SSD_P, SSD_N), F32),
        jax.ShapeDtypeStruct((bsz, HGRN_HEADS * HGRN_D, HGRN_D), F32),
    )
    out_specs = (
        pl.BlockSpec((tt, D_MODEL), lambda b, t: (b * nt + t, 0)),
        pl.BlockSpec((tt, ROW_TILES, LANE), lambda b, t: (b * nt + t, 0, 0)),
        pl.BlockSpec((N_EXPERTS, tt), lambda b, t: (0, b * nt + t)),
        pl.BlockSpec((1, CONV_W - 1, CONV_DIM), lambda b, t: (b, 0, 0)),
        pl.BlockSpec((1, SSD_HEADS * SSD_P, SSD_N), lambda b, t: (b, 0, 0)),
        pl.BlockSpec((1, HGRN_HEADS * HGRN_D, HGRN_D), lambda b, t: (b, 0, 0)),
    )
    scratch = [
        pltpu.VMEM((HALO + tt, PW), F32),
        pltpu.VMEM((tt, 2048), BF16),
        pltpu.VMEM((HGRN_HEADS * HGRN_D, HGRN_D), F32),
    ] + _chunk_scratch(SSD_CHUNK)
    return pl.pallas_call(
        functools.partial(_prompt_kernel, tt, nt),
        grid=(bsz, nt),
        in_specs=[pl.BlockSpec((1, tt, D_MODEL), lambda b, t: (b, t, 0))] + const_specs,
        out_specs=out_specs,
        out_shape=out_shape,
        scratch_shapes=scratch,
        compiler_params=pltpu.CompilerParams(dimension_semantics=("arbitrary", "arbitrary"),
                                             vmem_limit_bytes=VMEM_LIMIT),
        name="prompt_mixer",
    )(x, *consts)


def _sample_inproj_kernel(x_ref, n1_ref, win_ref, o_ref):
    _in_proj(x_ref[...], n1_ref, win_ref, o_ref, 0)


def _sample_inproj(x, n1, win, rows=256):
    n = x.shape[0]
    return pl.pallas_call(
        _sample_inproj_kernel,
        grid=(n // rows,),
        in_specs=[pl.BlockSpec((rows, D_MODEL), lambda i: (i, 0)), _const_spec(n1.shape), _const_spec(win.shape)],
        out_specs=pl.BlockSpec((rows, PW), lambda i: (i, 0)),
        out_shape=jax.ShapeDtypeStruct((n, PW), F32),
        compiler_params=pltpu.CompilerParams(dimension_semantics=("arbitrary",), vmem_limit_bytes=VMEM_LIMIT),
        name="sample_inproj",
    )(x, n1, win)


SAMPLE_L = 16


def _sample_step_kernel(tdec, pj_ref, cin_ref, sin_ref, hin_ref, cw, cb, dtb, alog, dskip, snw, hlb, hnw,
                        mix_ref, cout_ref, sout_ref, hout_ref,
                        pj, mix_s, hst, *scr):
    pj[...] = jnp.zeros_like(pj)
    pj[HALO - 3:HALO, X0:X0 + CONV_DIM] = cin_ref[0]
    pj[HALO:HALO + tdec, :] = pj_ref[0]
    sout_ref[0] = sin_ref[0]
    for hh in range(HGRN_HEADS):
        c = slice(128 * hh, 128 * hh + 128)
        hst[c, :] = hin_ref[0, c, :].T
    prm = (cw, cb, dtb, alog, dskip, snw, hlb, hnw)
    _mix_chunk(pj, HALO, SAMPLE_L, tdec, SAMPLE_L, sout_ref.at[0], hst, prm, scr, mix_s, 0)
    mix_ref[0] = mix_s[0:tdec, :]
    cout_ref[0] = pj[HALO + tdec - 3:HALO + tdec, X0:X0 + CONV_DIM]
    for hh in range(HGRN_HEADS):
        c = slice(128 * hh, 128 * hh + 128)
        hout_ref[0, c, :] = hst[c, :].T


def _sample_step(proj, conv_in, ssm_in, hgrn_in, consts):
    nseq, tdec, _ = proj.shape
    seq_spec = lambda shape: pl.BlockSpec((1,) + shape, lambda i: (i,) + (0,) * len(shape))
    scratch = [
        pltpu.VMEM((HALO + SAMPLE_L, PW), F32),
        pltpu.VMEM((SAMPLE_L, 2048), F32),
        pltpu.VMEM((HGRN_HEADS * HGRN_D, HGRN_D), F32),
    ] + _chunk_scratch(SAMPLE_L)
    return pl.pallas_call(
        functools.partial(_sample_step_kernel, tdec),
        grid=(nseq,),
        in_specs=[seq_spec((tdec, PW)), seq_spec((CONV_W - 1, CONV_DIM)), seq_spec((SSD_HEADS * SSD_P, SSD_N)),
                  seq_spec((HGRN_HEADS * HGRN_D, HGRN_D))] + [_const_spec(c.shape) for c in consts],
        out_specs=(seq_spec((tdec, 2048)), seq_spec((CONV_W - 1, CONV_DIM)), seq_spec((SSD_HEADS * SSD_P, SSD_N)),
                   seq_spec((HGRN_HEADS * HGRN_D, HGRN_D))),
        out_shape=(jax.ShapeDtypeStruct((nseq, tdec, 2048), F32),
                   jax.ShapeDtypeStruct((nseq, CONV_W - 1, CONV_DIM), F32),
                   jax.ShapeDtypeStruct((nseq, SSD_HEADS * SSD_P, SSD_N), F32),
                   jax.ShapeDtypeStruct((nseq, HGRN_HEADS * HGRN_D, HGRN_D), F32)),
        scratch_shapes=scratch,
        compiler_params=pltpu.CompilerParams(dimension_semantics=("arbitrary",), vmem_limit_bytes=VMEM_LIMIT),
        name="sample_step",
    )(proj, conv_in, ssm_in, hgrn_in, *consts)


def _sample_out_kernel(mix_ref, x_ref, wout_ref, n2_ref, wrt_ref, br_ref, x1_ref, xn2_ref, lg_ref):
    x1, xn2, lgt = _out_part(mix_ref[...].astype(BF16), x_ref[...], wout_ref, n2_ref, wrt_ref, br_ref)
    x1_ref[...] = x1
    _store_rows(xn2_ref, xn2)
    lg_ref[...] = lgt


def _sample_out(mix, x, wout, n2, wrt, br):
    n = x.shape[0]
    return pl.pallas_call(
        _sample_out_kernel,
        out_shape=(jax.ShapeDtypeStruct((n, D_MODEL), F32), jax.ShapeDtypeStruct((n, ROW_TILES, LANE), F32),
                   jax.ShapeDtypeStruct((N_EXPERTS, n), F32)),
        compiler_params=pltpu.CompilerParams(vmem_limit_bytes=VMEM_LIMIT),
        name="sample_out",
    )(mix, x, wout, n2, wrt, br)


ROUTE_TILE = 256


def _route_kernel(n_first, lga_ref, lgb_ref, g_ref, d_ref, cnt_ref, cnt, carry, pstart):
    ph = pl.program_id(0)
    i = pl.program_id(1)
    tl = ROUTE_TILE
    l = jnp.where(i < n_first, lga_ref[...], lgb_ref[...])
    eid = lax.broadcasted_iota(I32, (N_EXPERTS, tl), 0)
    hots, vals = [], []
    for _ in range(TOP_K):
        m = jnp.max(l, axis=0, keepdims=True)
        idx = jnp.min(jnp.where(l == m, eid, N_EXPERTS), axis=0, keepdims=True)
        hot = eid == idx
        hots.append(hot)
        vals.append(m)
        l = jnp.where(hot, -jnp.inf, l)
    ind = hots[0].astype(F32)
    for hot in hots[1:]:
        ind = ind + hot.astype(F32)
    tile_cnt = jnp.sum(ind, axis=1, keepdims=True)

    @pl.when(jnp.logical_and(ph == 0, i == 0))
    def _():
        cnt[...] = jnp.zeros_like(cnt)

    @pl.when(ph == 0)
    def _():
        cnt[...] = cnt[...] + tile_cnt

    @pl.when(jnp.logical_and(ph == 1, i == 0))
    def _():
        padded = jnp.floor((cnt[...] + (MOE_BLOCK - 1)) * (1.0 / MOE_BLOCK)) * MOE_BLOCK
        r = lax.broadcasted_iota(I32, (N_EXPERTS, N_EXPERTS), 0)
        c = lax.broadcasted_iota(I32, (N_EXPERTS, N_EXPERTS), 1)
        pend = jnp.dot((c <= r).astype(F32), padded, precision=HI, preferred_element_type=F32)
        pstart[...] = pend - padded
        carry[...] = jnp.zeros_like(carry)
        cnt_ref[...] = cnt[...]

    @pl.when(ph == 1)
    def _():
        rr = lax.broadcasted_iota(I32, (tl, tl), 0)
        cc = lax.broadcasted_iota(I32, (tl, tl), 1)
        before = _dot(ind.astype(BF16), (rr < cc).astype(BF16))
        base = before + carry[:, 0:1] + pstart[:, 0:1]
        den = jnp.exp(vals[0] - vals[0])
        for k in range(1, TOP_K):
            den = den + jnp.exp(vals[k] - vals[0])
        g_ref[...] = jnp.zeros_like(g_ref)
        for k in range(TOP_K):
            g_ref[k:k + 1, :] = jnp.exp(vals[k] - vals[0]) / den
            d_ref[k:k + 1, :] = jnp.sum(jnp.where(hots[k], base, 0.0), axis=0, keepdims=True).astype(I32)
        carry[...] = carry[...] + tile_cnt


def _two_source_maps(n_first, trailing=1):
    pad = (0,) * trailing
    return (lambda *idx: (jnp.minimum(idx[-1], n_first - 1),) + pad,
            lambda *idx: (jnp.maximum(idx[-1] - n_first, 0),) + pad)


def _route(lg_a, lg_b):
    ta = lg_a.shape[1] + lg_b.shape[1]
    nt = ta // ROUTE_TILE
    n_first = lg_a.shape[1] // ROUTE_TILE
    assert n_first * ROUTE_TILE == lg_a.shape[1] and nt * ROUTE_TILE == ta
    return pl.pallas_call(
        functools.partial(_route_kernel, n_first),
        grid=(2, nt),
        in_specs=[pl.BlockSpec((N_EXPERTS, ROUTE_TILE), lambda p, i: (0, jnp.minimum(i, n_first - 1))),
                  pl.BlockSpec((N_EXPERTS, ROUTE_TILE), lambda p, i: (0, jnp.maximum(i - n_first, 0)))],
        out_specs=(pl.BlockSpec((8, ROUTE_TILE), lambda p, i: (0, i * p)),
                   pl.BlockSpec((None, TOP_K, ROUTE_TILE), lambda p, i: (i * p, 0, 0)),
                   pl.BlockSpec((N_EXPERTS, LANE), lambda p, i: (0, 0))),
        out_shape=(jax.ShapeDtypeStruct((8, ta), F32), jax.ShapeDtypeStruct((nt, TOP_K, ROUTE_TILE), I32),
                   jax.ShapeDtypeStruct((N_EXPERTS, LANE), F32)),
        scratch_shapes=[pltpu.VMEM((N_EXPERTS, LANE), F32)] * 3,
        compiler_params=pltpu.CompilerParams(dimension_semantics=("arbitrary", "arbitrary")),
        name="route",
    )(lg_a, lg_b)


DMA_UNROLL = 2


def _row_copy(src, dst, sem):
    return pltpu.make_async_copy(src, dst, sem)


def _dispatch_kernel(n_first, pend_ref, padded_ref, nu_ref, d_ref, xa_ref, xb_ref, o_ref, zeros, sem):
    i = pl.program_id(0)
    tl = ROUTE_TILE
    nb = o_ref.shape[0] // MOE_BLOCK

    @pl.when(i == 0)
    def _():
        zeros[...] = jnp.zeros_like(zeros)

        def fills(e):
            return ((padded_ref[e] > 0, pend_ref[e] - MOE_BLOCK),
                    (nu_ref[0] + e < nb, (nu_ref[0] + e) * MOE_BLOCK))

        for e in range(N_EXPERTS):
            for cond, row in fills(e):
                @pl.when(cond)
                def _(row=row):
                    _row_copy(zeros, o_ref.at[pl.ds(row, MOE_BLOCK)], sem).start()
        for e in range(N_EXPERTS):
            for cond, row in fills(e):
                @pl.when(cond)
                def _(row=row):
                    _row_copy(zeros, o_ref.at[pl.ds(row, MOE_BLOCK)], sem).wait()

    def scatter_rows(x_ref):
        def start(j2, c):
            for u in range(DMA_UNROLL):
                j = j2 * DMA_UNROLL + u
                for k in range(TOP_K):
                    _row_copy(x_ref.at[j], o_ref.at[d_ref[k, j]], sem).start(priority=(u + k) % 2)
            return c

        lax.fori_loop(0, tl // DMA_UNROLL, start, 0)

        def wait(j2, c):
            for u in range(DMA_UNROLL):
                j = j2 * DMA_UNROLL + u
                for k in range(TOP_K):
                    _row_copy(x_ref.at[j], o_ref.at[d_ref[k, j]], sem).wait()
            return c

        lax.fori_loop(0, tl // DMA_UNROLL, wait, 0)

    @pl.when(i < n_first)
    def _():
        scatter_rows(xa_ref)

    @pl.when(i >= n_first)
    def _():
        scatter_rows(xb_ref)


def _dispatch(xa, xb, d3, pend, padded, n_used, n_rows):
    nt = (xa.shape[0] + xb.shape[0]) // ROUTE_TILE
    n_first = xa.shape[0] // ROUTE_TILE
    map_a, map_b = _two_source_maps(n_first, trailing=2)
    grid_spec = pltpu.PrefetchScalarGridSpec(
        num_scalar_prefetch=3,
        grid=(nt,),
        in_specs=[pl.BlockSpec((None, TOP_K, ROUTE_TILE), lambda i, *_: (i, 0, 0), memory_space=pltpu.SMEM),
                  pl.BlockSpec((ROUTE_TILE, ROW_TILES, LANE), lambda i, *_: map_a(i)),
                  pl.BlockSpec((ROUTE_TILE, ROW_TILES, LANE), lambda i, *_: map_b(i))],
        out_specs=pl.BlockSpec(memory_space=pl.ANY),
        scratch_shapes=[pltpu.VMEM((MOE_BLOCK, ROW_TILES, LANE), F32), pltpu.SemaphoreType.DMA],
    )
    return pl.pallas_call(
        functools.partial(_dispatch_kernel, n_first),
        grid_spec=grid_spec,
        out_shape=jax.ShapeDtypeStruct((n_rows, ROW_TILES, LANE), F32),
        compiler_params=pltpu.CompilerParams(dimension_semantics=("arbitrary",)),
        name="dispatch",
    )(pend, padded, n_used, d3, xa, xb)


def _expert_kernel(be_ref, nu_ref, x_ref, wg_ref, bg_ref, wu_ref, bu_ref, wd_ref, bd_ref, o_ref, wg_s, wu_s, wd_s,
                   lhs_s):
    i = pl.program_id(0)
    prev = be_ref[jnp.maximum(i - 1, 0)]
    fresh = jnp.logical_or(i == 0, be_ref[i] != prev)

    @pl.when(jnp.logical_and(i < nu_ref[0], fresh))
    def _():
        wg_s[...] = wg_ref[0].astype(BF16)
        wu_s[...] = wu_ref[0].astype(BF16)
        wd_s[...] = wd_ref[0].astype(BF16)

    @pl.when(i < nu_ref[0])
    def _():
        for s, slab in enumerate(_load_rows(x_ref)):
            lhs_s[:, LANE * s:LANE * s + LANE] = slab.astype(BF16)
        x = lhs_s[...]
        g = _dot(x, wg_s[...]) + bg_ref[0]
        u = _dot(x, wu_s[...]) + bu_ref[0]
        g = jnp.minimum(g, SWIGLU_LIMIT)
        u = jnp.clip(u, -SWIGLU_LIMIT, SWIGLU_LIMIT)
        act = g * jax.nn.sigmoid(SWIGLU_ALPHA * g) * (u + 1.0)
        _store_rows(o_ref, _dot(act.astype(BF16), wd_s[...]) + bd_ref[0])

    @pl.when(i >= nu_ref[0])
    def _():
        o_ref[...] = jnp.zeros_like(o_ref)


def _experts(xs, block_e, n_used, wg, bg, wu, bu, wd, bd):
    n_rows = xs.shape[0]
    nb = n_rows // MOE_BLOCK
    row_map = lambda i, be, nu: (jnp.maximum(jnp.minimum(i, nu[0] - 1), 0), 0, 0)
    w_map = lambda i, be, nu: (be[i], 0, 0)
    w_spec = pl.BlockSpec((1, D_MODEL, D_MODEL), w_map)
    b_spec = pl.BlockSpec((1, 1, D_MODEL), w_map)
    grid_spec = pltpu.PrefetchScalarGridSpec(
        num_scalar_prefetch=2,
        grid=(nb,),
        in_specs=[pl.BlockSpec((MOE_BLOCK, ROW_TILES, LANE), row_map), w_spec, b_spec, w_spec, b_spec, w_spec,
                  b_spec],
        out_specs=pl.BlockSpec((MOE_BLOCK, ROW_TILES, LANE), lambda i, be, nu: (i, 0, 0)),
        scratch_shapes=[pltpu.VMEM((D_MODEL, D_MODEL), BF16)] * 3 + [pltpu.VMEM((MOE_BLOCK, D_MODEL), BF16)],
    )
    return pl.pallas_call(
        _expert_kernel,
        grid_spec=grid_spec,
        out_shape=jax.ShapeDtypeStruct((n_rows, ROW_TILES, LANE), F32),
        compiler_params=pltpu.CompilerParams(dimension_semantics=("arbitrary",), vmem_limit_bytes=VMEM_LIMIT),
        name="experts",
    )(block_e, n_used, xs, wg, bg.reshape(N_EXPERTS, 1, D_MODEL), wu, bu.reshape(N_EXPERTS, 1, D_MODEL), wd,
      bd.reshape(N_EXPERTS, 1, D_MODEL))


def _combine_kernel(n_first, d_ref, x1a_ref, x1b_ref, g_ref, nf_ref, eo_ref, ya_ref, yb_ref, buf, y_s, sem):
    i = pl.program_id(0)
    tl = ROUTE_TILE

    def start(j2, c):
        for u in range(DMA_UNROLL):
            j = j2 * DMA_UNROLL + u
            for k in range(TOP_K):
                _row_copy(eo_ref.at[d_ref[k, j]], buf.at[k, j], sem).start(priority=(u + k) % 2)
        return c

    lax.fori_loop(0, tl // DMA_UNROLL, start, 0)
    sel = (lax.broadcasted_iota(I32, (8, LANE), 0) == lax.broadcasted_iota(I32, (8, LANE), 1)).astype(F32)
    gt = _dot_tn(g_ref[...], sel, precision=HI)

    def wait(j2, c):
        for u in range(DMA_UNROLL):
            j = j2 * DMA_UNROLL + u
            for k in range(TOP_K):
                _row_copy(eo_ref.at[d_ref[k, j]], buf.at[k, j], sem).wait()
        return c

    lax.fori_loop(0, tl // DMA_UNROLL, wait, 0)
    for k in range(TOP_K):
        for s, slab in enumerate(_load_rows(buf.at[k])):
            cols = slice(LANE * s, LANE * s + LANE)
            term = gt[:, k:k + 1] * slab
            y_s[:, cols] = term if k == 0 else y_s[:, cols] + term
    y = jnp.where(i < n_first, x1a_ref[...], x1b_ref[...]) + y_s[...]
    y = y * lax.rsqrt(jnp.mean(y * y, axis=-1, keepdims=True) + EPS) * nf_ref[...]

    @pl.when(i < n_first)
    def _():
        ya_ref[...] = y

    @pl.when(i >= n_first)
    def _():
        yb_ref[...] = y


def _combine(x1a, x1b, gates8, d3, eo3, nf):
    nt = (x1a.shape[0] + x1b.shape[0]) // ROUTE_TILE
    n_first = x1a.shape[0] // ROUTE_TILE
    map_a, map_b = _two_source_maps(n_first)
    row_specs = (pl.BlockSpec((ROUTE_TILE, D_MODEL), map_a), pl.BlockSpec((ROUTE_TILE, D_MODEL), map_b))
    grid_spec = pltpu.PrefetchScalarGridSpec(
        num_scalar_prefetch=0,
        grid=(nt,),
        in_specs=[pl.BlockSpec((None, TOP_K, ROUTE_TILE), lambda i: (i, 0, 0), memory_space=pltpu.SMEM),
                  *row_specs,
                  pl.BlockSpec((8, ROUTE_TILE), lambda i: (0, i)),
                  _const_spec(nf.shape),
                  pl.BlockSpec(memory_space=pl.ANY)],
        out_specs=row_specs,
        scratch_shapes=[pltpu.VMEM((TOP_K, ROUTE_TILE, ROW_TILES, LANE), F32),
                        pltpu.VMEM((ROUTE_TILE, D_MODEL), F32), pltpu.SemaphoreType.DMA],
    )
    return pl.pallas_call(
        functools.partial(_combine_kernel, n_first),
        grid_spec=grid_spec,
        out_shape=(jax.ShapeDtypeStruct(x1a.shape, F32), jax.ShapeDtypeStruct(x1b.shape, F32)),
        compiler_params=pltpu.CompilerParams(dimension_semantics=("arbitrary",), vmem_limit_bytes=VMEM_LIMIT),
        name="combine",
    )(d3, x1a, x1b, gates8, nf, eo3)


def _moe_and_norm(x1, xn2, lgt, wg, bg, wu, bu, wd, bd, nf):
    ta = x1[0].shape[0] + x1[1].shape[0]
    n_rows = (-(-(ta * TOP_K) // MOE_BLOCK) + N_EXPERTS) * MOE_BLOCK
    gates8, d3, cnt = _route(*lgt)
    counts = cnt[:, 0].astype(I32)
    padded = (counts + MOE_BLOCK - 1) // MOE_BLOCK * MOE_BLOCK
    pend = jnp.cumsum(padded)
    n_used = (pend[-1:] // MOE_BLOCK).astype(I32)
    block_row = jnp.arange(n_rows // MOE_BLOCK, dtype=I32) * MOE_BLOCK
    block_e = jnp.minimum(jnp.sum((pend[None, :] <= block_row[:, None]).astype(I32), axis=1), N_EXPERTS - 1)
    xs = _dispatch(*xn2, d3, pend.astype(I32), padded.astype(I32), n_used, n_rows)
    eo = _experts(xs, block_e, n_used, wg, bg, wu, bu, wd, bd)
    return _combine(*x1, gates8, d3, eo, nf)


def _prep_consts(norm1_w, w_in, conv_w, conv_b, dt_bias, A_log, D_skip, ssd_norm_w, hgrn_lower_bound, hgrn_norm_w,
                 w_out, norm2_w, w_router, b_router):
    row = lambda v: v.reshape(1, -1).astype(F32)
    pad_lane = lambda v: jnp.pad(v.reshape(1, -1).astype(F32), ((0, 0), (0, LANE - v.size)))
    win = jnp.concatenate([w_in[:, :2560], w_in[:, 2576:6672], w_in[:, 2560:2576],
                           jnp.zeros((D_MODEL, LANE - SSD_HEADS), w_in.dtype)], axis=1).astype(BF16)
    dskip = jnp.repeat(D_skip.astype(F32), SSD_P).reshape(1, -1)
    return (row(norm1_w), win, conv_w.astype(F32), row(conv_b), pad_lane(dt_bias), pad_lane(A_log), dskip,
            row(ssd_norm_w), hgrn_lower_bound.astype(F32), row(hgrn_norm_w), w_out.astype(BF16), row(norm2_w),
            w_router.T.astype(F32), b_router.reshape(-1, 1).astype(F32))


def kernel(x_prompt, x_sample, state_conv, state_ssm, state_hgrn, norm1_w, w_in, conv_w, conv_b, dt_bias, A_log, D_skip, ssd_norm_w, hgrn_lower_bound, hgrn_norm_w, w_out, norm2_w, w_router, b_router, w_gate, b_gate, w_up, b_up, w_down, b_down, norm_f_w):
    depth = w_in.shape[0]
    assert depth == 1
    bsz, seq, _ = x_prompt.shape
    nseq, tdec, _ = x_sample.shape
    consts = _prep_consts(norm1_w[0], w_in[0], conv_w[0], conv_b[0], dt_bias[0], A_log[0], D_skip[0], ssd_norm_w[0],
                          hgrn_lower_bound, hgrn_norm_w[0], w_out[0], norm2_w[0], w_router[0], b_router[0])
    n1, win, cw, cb, dtb, alog, dskip, snw, hlb, hnw, wout, n2, wrt, br = consts
    chunk_consts = (cw, cb, dtb, alog, dskip, snw, hlb, hnw)

    x1_p, xn2_p, lg_p, conv_p, ssm_p, hgrn_p = _prompt_mixer(x_prompt, consts)

    xs_flat = x_sample.reshape(nseq * tdec, D_MODEL)
    proj_s = _sample_inproj(xs_flat, n1, win).reshape(nseq, tdec, PW)
    mix_s, conv_s, ssm_s, hgrn_s = _sample_step(
        proj_s, state_conv[0], state_ssm[0].reshape(nseq, SSD_HEADS * SSD_P, SSD_N),
        state_hgrn[0].reshape(nseq, HGRN_HEADS * HGRN_D, HGRN_D), chunk_consts)
    x1_s, xn2_s, lg_s = _sample_out(mix_s.reshape(nseq * tdec, 2048), xs_flat, wout, n2, wrt, br)

    y_p, y_s = _moe_and_norm((x1_p, x1_s), (xn2_p, xn2_s), (lg_p, lg_s), w_gate[0], b_gate[0], w_up[0], b_up[0],
                             w_down[0], b_down[0], norm_f_w.reshape(1, -1).astype(F32))
    y_prompt = y_p.reshape(bsz, seq, D_MODEL)
    y_sample = y_s.reshape(nseq, tdec, D_MODEL)
    return (y_prompt, y_sample,
            conv_p[None], ssm_p.reshape(1, bsz, SSD_HEADS, SSD_P, SSD_N),
            hgrn_p.reshape(1, bsz, HGRN_HEADS, HGRN_D, HGRN_D),
            conv_s[None], ssm_s.reshape(1, nseq, SSD_HEADS, SSD_P, SSD_N),
            hgrn_s.reshape(1, nseq, HGRN_HEADS, HGRN_D, HGRN_D))
```

```python
import functools

import jax
import jax.numpy as jnp
from jax import lax
from jax.experimental import pallas as pl
from jax.experimental.pallas import tpu as pltpu

F32 = jnp.float32
BF16 = jnp.bfloat16
I32 = jnp.int32
HI = lax.Precision.HIGHEST

D_MODEL = 1024
SSD_HEADS = 16
SSD_P = 64
SSD_N = 128
CONV_DIM = 1536
CONV_W = 4
HGRN_HEADS = 8
HGRN_D = 128
N_EXPERTS = 32
TOP_K = 4
MOE_BLOCK = 256
SWIGLU_LIMIT = 7.0
SWIGLU_ALPHA = 1.702
EPS = 1e-6

Z0 = 0
X0 = 1024
Q0 = 2560
F0 = 3584
I0 = 4608
G0 = 5632
DT0 = 6656
PW = 6784
LANE = 128
HALO = 8
SSD_CHUNK = 128
HGRN_CHUNK = 16
MAX_SUB = SSD_CHUNK // HGRN_CHUNK
VMEM_LIMIT = 56 * 1024 * 1024


def _silu(x):
    return x * jax.nn.sigmoid(x)


def _softplus(x):
    return jnp.maximum(x, 0.0) + jnp.log1p(jnp.exp(-jnp.abs(x)))


def _dot(a, b):
    return jnp.dot(a, b, preferred_element_type=F32)


def _dot_nt(a, b, precision=None):
    return lax.dot_general(a, b, (((1,), (1,)), ((), ())), precision=precision, preferred_element_type=F32)


def _dot_tn(a, b, precision=None):
    return lax.dot_general(a, b, (((0,), (0,)), ((), ())), precision=precision, preferred_element_type=F32)


def _split3(x):
    hi = x.astype(BF16)
    r1 = x - hi.astype(F32)
    mid = r1.astype(BF16)
    lo = (r1 - mid.astype(F32)).astype(BF16)
    return hi, mid, lo


def _sel_dot(sel, x):
    return sum(_dot(sel, p) for p in _split3(x))


def _sel_dot_tn(x, sel):
    return sum(_dot_tn(p, sel) for p in _split3(x))


def _in_proj(x, n1_ref, win_ref, pj, r0):
    rows = x.shape[0]
    ms = jnp.mean(x * x, axis=-1, keepdims=True)
    h = (x * lax.rsqrt(ms + EPS) * n1_ref[...]).astype(BF16)
    for c0 in range(0, PW, 512):
        c1 = min(c0 + 512, PW)
        pj[r0:r0 + rows, c0:c1] = _dot(h, win_ref[:, c0:c1])


def _mix_chunk(pj, r0, L, nv, LH, ssm, hst, prm, scr, mix_s, m0):
    cw, cb, dtb, alog, dskip, snw, hlb, hnw = prm
    y_s, b_s, kk_s, r_s, q_s, qb_s, v_s, kend_s = scr

    def conv_cols(c0, c1):
        acc = cb[:, c0:c1]
        for k in range(CONV_W):
            acc = acc + pj[r0 - 3 + k:r0 - 3 + k + L, X0 + c0:X0 + c1] * cw[k:k + 1, c0:c1]
        return _silu(acc)

    rowi = lax.broadcasted_iota(I32, (L, L), 0)
    coli = lax.broadcasted_iota(I32, (L, L), 1)
    causal = coli <= rowi
    rid = lax.broadcasted_iota(I32, (L, LANE), 0)
    lane = lax.broadcasted_iota(I32, (L, LANE), 1)
    lo = lane < SSD_P

    dt = _softplus(pj[r0:r0 + L, DT0:DT0 + LANE] + dtb[...])
    if nv < L:
        dt = jnp.where(rid < nv, dt, 0.0)
    a = dt * (-jnp.exp(alog[...]))
    acum = _sel_dot(causal.astype(BF16), a)
    acum_t = _sel_dot_tn(a, (rowi <= coli).astype(BF16))
    dt_t = _sel_dot_tn(dt, (rowi == coli).astype(BF16))
    eac = jnp.exp(acum)
    last = acum[L - 1:L, :]
    elast = jnp.exp(last)
    tail = jnp.exp(last - acum) * dt

    bm = [conv_cols(1024 + 128 * g, 1152 + 128 * g).astype(BF16) for g in range(2)]
    cm = [conv_cols(1280 + 128 * g, 1408 + 128 * g).astype(BF16) for g in range(2)]
    gcb = [_dot_nt(cm[g], bm[g]) for g in range(2)]
    rowh = lax.broadcasted_iota(I32, (LANE, LANE), 0) < SSD_P

    for j in range(SSD_HEADS // 2):
        g = j // 4
        cols = slice(128 * j, 128 * j + 128)
        xs = conv_cols(128 * j, 128 * j + 128)
        xsb = xs.astype(BF16)
        ys = []
        for h in (2 * j, 2 * j + 1):
            seg = acum[:, h:h + 1] - acum_t[h:h + 1, :]
            dec = jnp.exp(jnp.where(causal, seg, -jnp.inf))
            mh = (gcb[g] * dec * dt_t[h:h + 1, :]).astype(BF16)
            ys.append(_dot(mh, xsb))
        y = jnp.where(lo, ys[0], ys[1])
        hp = ssm[128 * j:128 * j + 128, :]
        yoff = _dot_nt(cm[g], hp.astype(BF16))
        y = y + yoff * jnp.where(lo, eac[:, 2 * j:2 * j + 1], eac[:, 2 * j + 1:2 * j + 2])
        y = y + dskip[:, cols] * xs
        y_s[0:L, cols] = y
        xt = (xs * jnp.where(lo, tail[:, 2 * j:2 * j + 1], tail[:, 2 * j + 1:2 * j + 2])).astype(BF16)
        upd = _dot_tn(xt, bm[g])
        el = jnp.where(rowh, elast[:, 2 * j:2 * j + 1], elast[:, 2 * j + 1:2 * j + 2])
        ssm[128 * j:128 * j + 128, :] = el * hp + upd

    yz = y_s[0:L, :] * _silu(pj[r0:r0 + L, Z0:Z0 + 1024])
    for g in range(2):
        sl = slice(512 * g, 512 * g + 512)
        part = yz[:, sl]
        ms = jnp.mean(part * part, axis=-1, keepdims=True)
        mix_s[m0:m0 + L, sl] = (part * lax.rsqrt(ms + EPS) * snw[:, sl]).astype(mix_s.dtype)

    n_sub = L // LH
    AH = min(L, 2 * LH)
    n_att = L // AH
    r0v = hlb[0:1, :]
    r1v = hlb[1:2, :]
    mx = jnp.maximum(r0v, r1v)
    e0 = jnp.exp(r0v - mx)
    lb = e0 / (e0 + jnp.exp(r1v - mx))
    rid_w = lax.broadcasted_iota(I32, (L, 1024), 0)
    f = lb + (1.0 - lb) * jax.nn.sigmoid(pj[r0:r0 + L, F0:F0 + 1024])
    if nv < L:
        f = jnp.where(rid_w < nv, f, 1.0)
    kk = 1.0 - f
    blk = jnp.logical_and(causal, (rowi // LH) == (coli // LH)).astype(F32)
    b = _sel_dot(blk.astype(BF16), jnp.log(f))
    q = _silu(pj[r0:r0 + L, Q0:Q0 + 1024])
    base = jnp.zeros((1, 1024), F32)
    for i in range(n_sub):
        rows = slice(LH * i, LH * i + LH)
        bi = b[rows, :] + base
        b_s[rows, :] = bi
        qb_s[rows, :] = (q[rows, :] * jnp.exp(bi)).astype(BF16)
        base = bi[LH - 1:LH, :]
    for i in range(n_att):
        rows = slice(AH * i, AH * i + AH)
        mid = b_s[AH * i + AH // 2 - 1:AH * i + AH // 2, :]
        r_s[i:i + 1, :] = mid
        q_s[rows, :] = (q[rows, :] * jnp.exp(b_s[rows, :] - mid)).astype(BF16)
    kk_s[0:L, :] = kk
    v_s[0:L, :] = pj[r0:r0 + L, I0:I0 + 1024].astype(BF16)
    kend_s[0:L, :] = (kk * jnp.exp(base - b_s[0:L, :])).astype(BF16)
    ebl = jnp.exp(base)
    colq = lax.broadcasted_iota(I32, (AH, L), 1)
    rowq = lax.broadcasted_iota(I32, (AH, L), 0)

    for hh in range(HGRN_HEADS):
        c = slice(128 * hh, 128 * hh + 128)
        bh = b_s[0:L, c]
        kkh = kk_s[0:L, c]
        parts = []
        for i in range(n_att):
            n_keys = AH * i + AH
            ktil = (kkh[0:n_keys, :] * jnp.exp(r_s[i:i + 1, c] - bh[0:n_keys, :])).astype(BF16)
            if n_keys < L:
                ktil = jnp.concatenate([ktil, jnp.zeros((L - n_keys, LANE), BF16)], axis=0)
            att = _dot_nt(q_s[AH * i:AH * i + AH, c], ktil)
            parts.append(jnp.where(colq <= rowq + AH * i, att, 0.0).astype(BF16))
        att_all = parts[0] if n_att == 1 else jnp.concatenate(parts, axis=0)
        st = hst[c, :]
        oh = _dot(att_all, v_s[0:L, c]) + _dot_nt(qb_s[0:L, c], st.astype(BF16))
        hst[c, :] = ebl[:, c] * st + _dot_tn(v_s[0:L, c], kend_s[0:L, c])
        ms = jnp.mean(oh * oh, axis=-1, keepdims=True)
        on = oh * lax.rsqrt(ms + EPS) * hnw[:, c]
        mix_s[m0:m0 + L, 1024 + 128 * hh:1152 + 128 * hh] = (
            on * _silu(pj[r0:r0 + L, G0 + 128 * hh:G0 + 128 * hh + 128])).astype(mix_s.dtype)


def _chunk_scratch(rows):
    wide = lambda dt: pltpu.VMEM((rows, 1024), dt)
    return [wide(F32), wide(F32), wide(F32), pltpu.VMEM((MAX_SUB, 1024), F32), wide(BF16), wide(BF16), wide(BF16),
            wide(BF16)]


ROW_TILES = D_MODEL // LANE


def _store_rows(ref, val):
    for s in range(ROW_TILES):
        ref[:, s, :] = val[:, LANE * s:LANE * s + LANE]


def _load_rows(ref):
    planes = jnp.swapaxes(ref[...], 0, 1)
    return [planes[s] for s in range(ROW_TILES)]


def _out_part(mix, x, wout_ref, n2_ref, wrt_ref, br_ref):
    x1 = x + _dot(mix, wout_ref[...])
    ms = jnp.mean(x1 * x1, axis=-1, keepdims=True)
    xn2 = x1 * lax.rsqrt(ms + EPS) * n2_ref[...]
    lgt = _dot_nt(wrt_ref[...], xn2, precision=HI) + br_ref[...]
    return x1, xn2, lgt


def _prompt_kernel(tt, nt, x_ref, n1_ref, win_ref, cw, cb, dtb, alog, dskip, snw, hlb, hnw, wout_ref, n2_ref, wrt_ref,
                   br_ref, x1_ref, xn2_ref, lg_ref, conv_ref, ssm_ref, hgrn_ref,
                   pj, mix_s, hst, *scr):
    t = pl.program_id(1)

    @pl.when(t == 0)
    def _():
        pj[0:HALO, :] = jnp.zeros((HALO, PW), F32)
        ssm_ref[0] = jnp.zeros((SSD_HEADS * SSD_P, SSD_N), F32)
        hst[...] = jnp.zeros_like(hst)

    x = x_ref[0]
    _in_proj(x, n1_ref, win_ref, pj, HALO)
    prm = (cw, cb, dtb, alog, dskip, snw, hlb, hnw)
    n_scr = len(scr) // (tt // SSD_CHUNK)
    for c in range(tt // SSD_CHUNK):
        _mix_chunk(pj, HALO + SSD_CHUNK * c, SSD_CHUNK, SSD_CHUNK, HGRN_CHUNK, ssm_ref.at[0], hst, prm,
                   scr[n_scr * c:n_scr * c + n_scr], mix_s, SSD_CHUNK * c)
    x1, xn2, lgt = _out_part(mix_s[...], x, wout_ref, n2_ref, wrt_ref, br_ref)
    x1_ref[...] = x1
    _store_rows(xn2_ref, xn2)
    lg_ref[...] = lgt

    @pl.when(t == nt - 1)
    def _():
        conv_ref[0] = pj[HALO + tt - 3:HALO + tt, X0:X0 + CONV_DIM]
        for hh in range(HGRN_HEADS):
            c = slice(128 * hh, 128 * hh + 128)
            hgrn_ref[0, c, :] = hst[c, :].T

    pj[0:HALO, X0:X0 + CONV_DIM] = pj[tt:tt + HALO, X0:X0 + CONV_DIM]


def _const_spec(shape):
    nd = len(shape)
    return pl.BlockSpec(shape, lambda *_: (0,) * nd)


def _prompt_mixer(x, consts, tt=256):
    bsz, seq, _ = x.shape
    nt = seq // tt
    ta = bsz * seq
    const_specs = [_const_spec(c.shape) for c in consts]
    out_shape = (
        jax.ShapeDtypeStruct((ta, D_MODEL), F32),
        jax.ShapeDtypeStruct((ta, ROW_TILES, LANE), F32),
        jax.ShapeDtypeStruct((N_EXPERTS, ta), F32),
        jax.ShapeDtypeStruct((bsz, CONV_W - 1, CONV_DIM), F32),
        jax.ShapeDtypeStruct((bsz, SSD_HEADS * SSD_P, SSD_N), F32),
        jax.ShapeDtypeStruct((bsz, HGRN_HEADS * HGRN_D, HGRN_D), F32),
    )
    out_specs = (
        pl.BlockSpec((tt, D_MODEL), lambda b, t: (b * nt + t, 0)),
        pl.BlockSpec((tt, ROW_TILES, LANE), lambda b, t: (b * nt + t, 0, 0)),
        pl.BlockSpec((N_EXPERTS, tt), lambda b, t: (0, b * nt + t)),
        pl.BlockSpec((1, CONV_W - 1, CONV_DIM), lambda b, t: (b, 0, 0)),
        pl.BlockSpec((1, SSD_HEADS * SSD_P, SSD_N), lambda b, t: (b, 0, 0)),
        pl.BlockSpec((1, HGRN_HEADS * HGRN_D, HGRN_D), lambda b, t: (b, 0, 0)),
    )
    scratch = [
        pltpu.VMEM((HALO + tt, PW), F32),
        pltpu.VMEM((tt, 2048), BF16),
        pltpu.VMEM((HGRN_HEADS * HGRN_D, HGRN_D), F32),
    ] + _chunk_scratch(SSD_CHUNK) * (tt // SSD_CHUNK)
    return pl.pallas_call(
        functools.partial(_prompt_kernel, tt, nt),
        grid=(bsz, nt),
        in_specs=[pl.BlockSpec((1, tt, D_MODEL), lambda b, t: (b, t, 0))] + const_specs,
        out_specs=out_specs,
        out_shape=out_shape,
        scratch_shapes=scratch,
        compiler_params=pltpu.CompilerParams(dimension_semantics=("arbitrary", "arbitrary"),
                                             vmem_limit_bytes=VMEM_LIMIT),
        name="prompt_mixer",
    )(x, *consts)


def _sample_inproj_kernel(x_ref, n1_ref, win_ref, o_ref):
    _in_proj(x_ref[...], n1_ref, win_ref, o_ref, 0)


def _sample_inproj(x, n1, win, rows=256):
    n = x.shape[0]
    return pl.pallas_call(
        _sample_inproj_kernel,
        grid=(n // rows,),
        in_specs=[pl.BlockSpec((rows, D_MODEL), lambda i: (i, 0)), _const_spec(n1.shape), _const_spec(win.shape)],
        out_specs=pl.BlockSpec((rows, PW), lambda i: (i, 0)),
        out_shape=jax.ShapeDtypeStruct((n, PW), F32),
        compiler_params=pltpu.CompilerParams(dimension_semantics=("arbitrary",), vmem_limit_bytes=VMEM_LIMIT),
        name="sample_inproj",
    )(x, n1, win)


SAMPLE_L = 16


def _sample_step_kernel(tdec, pj_ref, cin_ref, sin_ref, hin_ref, cw, cb, dtb, alog, dskip, snw, hlb, hnw,
                        mix_ref, cout_ref, sout_ref, hout_ref,
                        pj, mix_s, hst, *scr):
    pj[...] = jnp.zeros_like(pj)
    pj[HALO - 3:HALO, X0:X0 + CONV_DIM] = cin_ref[0]
    pj[HALO:HALO + tdec, :] = pj_ref[0]
    sout_ref[0] = sin_ref[0]
    for hh in range(HGRN_HEADS):
        c = slice(128 * hh, 128 * hh + 128)
        hst[c, :] = hin_ref[0, c, :].T
    prm = (cw, cb, dtb, alog, dskip, snw, hlb, hnw)
    _mix_chunk(pj, HALO, SAMPLE_L, tdec, SAMPLE_L, sout_ref.at[0], hst, prm, scr, mix_s, 0)
    mix_ref[0] = mix_s[0:tdec, :]
    cout_ref[0] = pj[HALO + tdec - 3:HALO + tdec, X0:X0 + CONV_DIM]
    for hh in range(HGRN_HEADS):
        c = slice(128 * hh, 128 * hh + 128)
        hout_ref[0, c, :] = hst[c, :].T


def _sample_step(proj, conv_in, ssm_in, hgrn_in, consts):
    nseq, tdec, _ = proj.shape
    seq_spec = lambda shape: pl.BlockSpec((1,) + shape, lambda i: (i,) + (0,) * len(shape))
    scratch = [
        pltpu.VMEM((HALO + SAMPLE_L, PW), F32),
        pltpu.VMEM((SAMPLE_L, 2048), F32),
        pltpu.VMEM((HGRN_HEADS * HGRN_D, HGRN_D), F32),
    ] + _chunk_scratch(SAMPLE_L)
    return pl.pallas_call(
        functools.partial(_sample_step_kernel, tdec),
        grid=(nseq,),
        in_specs=[seq_spec((tdec, PW)), seq_spec((CONV_W - 1, CONV_DIM)), seq_spec((SSD_HEADS * SSD_P, SSD_N)),
                  seq_spec((HGRN_HEADS * HGRN_D, HGRN_D))] + [_const_spec(c.shape) for c in consts],
        out_specs=(seq_spec((tdec, 2048)), seq_spec((CONV_W - 1, CONV_DIM)), seq_spec((SSD_HEADS * SSD_P, SSD_N)),
                   seq_spec((HGRN_HEADS * HGRN_D, HGRN_D))),
        out_shape=(jax.ShapeDtypeStruct((nseq, tdec, 2048), F32),
                   jax.ShapeDtypeStruct((nseq, CONV_W - 1, CONV_DIM), F32),
                   jax.ShapeDtypeStruct((nseq, SSD_HEADS * SSD_P, SSD_N), F32),
                   jax.ShapeDtypeStruct((nseq, HGRN_HEADS * HGRN_D, HGRN_D), F32)),
        scratch_shapes=scratch,
        compiler_params=pltpu.CompilerParams(dimension_semantics=("arbitrary",), vmem_limit_bytes=VMEM_LIMIT),
        name="sample_step",
    )(proj, conv_in, ssm_in, hgrn_in, *consts)


def _sample_out_kernel(mix_ref, x_ref, wout_ref, n2_ref, wrt_ref, br_ref, x1_ref, xn2_ref, lg_ref):
    x1, xn2, lgt = _out_part(mix_ref[...].astype(BF16), x_ref[...], wout_ref, n2_ref, wrt_ref, br_ref)
    x1_ref[...] = x1
    _store_rows(xn2_ref, xn2)
    lg_ref[...] = lgt


def _sample_out(mix, x, wout, n2, wrt, br):
    n = x.shape[0]
    return pl.pallas_call(
        _sample_out_kernel,
        out_shape=(jax.ShapeDtypeStruct((n, D_MODEL), F32), jax.ShapeDtypeStruct((n, ROW_TILES, LANE), F32),
                   jax.ShapeDtypeStruct((N_EXPERTS, n), F32)),
        compiler_params=pltpu.CompilerParams(vmem_limit_bytes=VMEM_LIMIT),
        name="sample_out",
    )(mix, x, wout, n2, wrt, br)


ROUTE_TILE = 256


def _route_kernel(n_first, lga_ref, lgb_ref, g_ref, d_ref, cnt_ref, cnt, carry, pstart):
    ph = pl.program_id(0)
    i = pl.program_id(1)
    tl = ROUTE_TILE
    l = jnp.where(i < n_first, lga_ref[...], lgb_ref[...])
    eid = lax.broadcasted_iota(I32, (N_EXPERTS, tl), 0)
    hots, vals = [], []
    for _ in range(TOP_K):
        m = jnp.max(l, axis=0, keepdims=True)
        idx = jnp.min(jnp.where(l == m, eid, N_EXPERTS), axis=0, keepdims=True)
        hot = eid == idx
        hots.append(hot)
        vals.append(m)
        l = jnp.where(hot, -jnp.inf, l)
    ind = hots[0].astype(F32)
    for hot in hots[1:]:
        ind = ind + hot.astype(F32)
    tile_cnt = jnp.sum(ind, axis=1, keepdims=True)

    @pl.when(jnp.logical_and(ph == 0, i == 0))
    def _():
        cnt[...] = jnp.zeros_like(cnt)

    @pl.when(ph == 0)
    def _():
        cnt[...] = cnt[...] + tile_cnt

    @pl.when(jnp.logical_and(ph == 1, i == 0))
    def _():
        padded = jnp.floor((cnt[...] + (MOE_BLOCK - 1)) * (1.0 / MOE_BLOCK)) * MOE_BLOCK
        r = lax.broadcasted_iota(I32, (N_EXPERTS, N_EXPERTS), 0)
        c = lax.broadcasted_iota(I32, (N_EXPERTS, N_EXPERTS), 1)
        pend = jnp.dot((c <= r).astype(F32), padded, precision=HI, preferred_element_type=F32)
        pstart[...] = pend - padded
        carry[...] = jnp.zeros_like(carry)
        cnt_ref[...] = cnt[...]

    @pl.when(ph == 1)
    def _():
        rr = lax.broadcasted_iota(I32, (tl, tl), 0)
        cc = lax.broadcasted_iota(I32, (tl, tl), 1)
        before = _dot(ind.astype(BF16), (rr < cc).astype(BF16))
        base = before + carry[:, 0:1] + pstart[:, 0:1]
        den = jnp.exp(vals[0] - vals[0])
        for k in range(1, TOP_K):
            den = den + jnp.exp(vals[k] - vals[0])
        g_ref[...] = jnp.zeros_like(g_ref)
        for k in range(TOP_K):
            g_ref[k:k + 1, :] = jnp.exp(vals[k] - vals[0]) / den
            d_ref[k:k + 1, :] = jnp.sum(jnp.where(hots[k], base, 0.0), axis=0, keepdims=True).astype(I32)
        carry[...] = carry[...] + tile_cnt


def _two_source_maps(n_first, trailing=1):
    pad = (0,) * trailing
    return (lambda *idx: (jnp.minimum(idx[-1], n_first - 1),) + pad,
            lambda *idx: (jnp.maximum(idx[-1] - n_first, 0),) + pad)


def _route(lg_a, lg_b):
    ta = lg_a.shape[1] + lg_b.shape[1]
    nt = ta // ROUTE_TILE
    n_first = lg_a.shape[1] // ROUTE_TILE
    assert n_first * ROUTE_TILE == lg_a.shape[1] and nt * ROUTE_TILE == ta
    return pl.pallas_call(
        functools.partial(_route_kernel, n_first),
        grid=(2, nt),
        in_specs=[pl.BlockSpec((N_EXPERTS, ROUTE_TILE), lambda p, i: (0, jnp.minimum(i, n_first - 1))),
                  pl.BlockSpec((N_EXPERTS, ROUTE_TILE), lambda p, i: (0, jnp.maximum(i - n_first, 0)))],
        out_specs=(pl.BlockSpec((8, ROUTE_TILE), lambda p, i: (0, i * p)),
                   pl.BlockSpec((None, TOP_K, ROUTE_TILE), lambda p, i: (i * p, 0, 0)),
                   pl.BlockSpec((N_EXPERTS, LANE), lambda p, i: (0, 0))),
        out_shape=(jax.ShapeDtypeStruct((8, ta), F32), jax.ShapeDtypeStruct((nt, TOP_K, ROUTE_TILE), I32),
                   jax.ShapeDtypeStruct((N_EXPERTS, LANE), F32)),
        scratch_shapes=[pltpu.VMEM((N_EXPERTS, LANE), F32)] * 3,
        compiler_params=pltpu.CompilerParams(dimension_semantics=("arbitrary", "arbitrary")),
        name="route",
    )(lg_a, lg_b)


DMA_UNROLL = 2


def _row_copy(src, dst, sem):
    return pltpu.make_async_copy(src, dst, sem)


def _dispatch_kernel(n_first, pend_ref, padded_ref, nu_ref, d_ref, xa_ref, xb_ref, o_ref, zeros, sem):
    i = pl.program_id(0)
    tl = ROUTE_TILE
    nb = o_ref.shape[0] // MOE_BLOCK

    @pl.when(i == 0)
    def _():
        zeros[...] = jnp.zeros_like(zeros)

        def fills(e):
            return ((padded_ref[e] > 0, pend_ref[e] - MOE_BLOCK),
                    (nu_ref[0] + e < nb, (nu_ref[0] + e) * MOE_BLOCK))

        for e in range(N_EXPERTS):
            for cond, row in fills(e):
                @pl.when(cond)
                def _(row=row):
                    _row_copy(zeros, o_ref.at[pl.ds(row, MOE_BLOCK)], sem).start()
        for e in range(N_EXPERTS):
            for cond, row in fills(e):
                @pl.when(cond)
                def _(row=row):
                    _row_copy(zeros, o_ref.at[pl.ds(row, MOE_BLOCK)], sem).wait()

    def scatter_rows(x_ref):
        def start(j2, c):
            for u in range(DMA_UNROLL):
                j = j2 * DMA_UNROLL + u
                for k in range(TOP_K):
                    _row_copy(x_ref.at[j], o_ref.at[d_ref[k, j]], sem).start(priority=(u + k) % 2)
            return c

        lax.fori_loop(0, tl // DMA_UNROLL, start, 0)

        def wait(j2, c):
            for u in range(DMA_UNROLL):
                j = j2 * DMA_UNROLL + u
                for k in range(TOP_K):
                    _row_copy(x_ref.at[j], o_ref.at[d_ref[k, j]], sem).wait()
            return c

        lax.fori_loop(0, tl // DMA_UNROLL, wait, 0)

    @pl.when(i < n_first)
    def _():
        scatter_rows(xa_ref)

    @pl.when(i >= n_first)
    def _():
        scatter_rows(xb_ref)


def _dispatch(xa, xb, d3, pend, padded, n_used, n_rows):
    nt = (xa.shape[0] + xb.shape[0]) // ROUTE_TILE
    n_first = xa.shape[0] // ROUTE_TILE
    map_a, map_b = _two_source_maps(n_first, trailing=2)
    grid_spec = pltpu.PrefetchScalarGridSpec(
        num_scalar_prefetch=3,
        grid=(nt,),
        in_specs=[pl.BlockSpec((None, TOP_K, ROUTE_TILE), lambda i, *_: (i, 0, 0), memory_space=pltpu.SMEM),
                  pl.BlockSpec((ROUTE_TILE, ROW_TILES, LANE), lambda i, *_: map_a(i)),
                  pl.BlockSpec((ROUTE_TILE, ROW_TILES, LANE), lambda i, *_: map_b(i))],
        out_specs=pl.BlockSpec(memory_space=pl.ANY),
        scratch_shapes=[pltpu.VMEM((MOE_BLOCK, ROW_TILES, LANE), F32), pltpu.SemaphoreType.DMA],
    )
    return pl.pallas_call(
        functools.partial(_dispatch_kernel, n_first),
        grid_spec=grid_spec,
        out_shape=jax.ShapeDtypeStruct((n_rows, ROW_TILES, LANE), F32),
        compiler_params=pltpu.CompilerParams(dimension_semantics=("arbitrary",)),
        name="dispatch",
    )(pend, padded, n_used, d3, xa, xb)


def _expert_kernel(be_ref, nu_ref, x_ref, wg_ref, bg_ref, wu_ref, bu_ref, wd_ref, bd_ref, o_ref, wg_s, wu_s, wd_s,
                   lhs_s):
    i = pl.program_id(0)
    prev = be_ref[jnp.maximum(i - 1, 0)]
    fresh = jnp.logical_or(i == 0, be_ref[i] != prev)

    @pl.when(jnp.logical_and(i < nu_ref[0], fresh))
    def _():
        wg_s[...] = wg_ref[0].astype(BF16)
        wu_s[...] = wu_ref[0].astype(BF16)
        wd_s[...] = wd_ref[0].astype(BF16)

    @pl.when(i < nu_ref[0])
    def _():
        for s, slab in enumerate(_load_rows(x_ref)):
            lhs_s[:, LANE * s:LANE * s + LANE] = slab.astype(BF16)
        half = MOE_BLOCK // 2
        for r in range(0, MOE_BLOCK, half):
            x = lhs_s[r:r + half, :]
            g = _dot(x, wg_s[...]) + bg_ref[0]
            u = _dot(x, wu_s[...]) + bu_ref[0]
            g = jnp.minimum(g, SWIGLU_LIMIT)
            u = jnp.clip(u, -SWIGLU_LIMIT, SWIGLU_LIMIT)
            act = g * jax.nn.sigmoid(SWIGLU_ALPHA * g) * (u + 1.0)
            _store_rows(o_ref.at[r:r + half], _dot(act.astype(BF16), wd_s[...]) + bd_ref[0])

    @pl.when(i >= nu_ref[0])
    def _():
        o_ref[...] = jnp.zeros_like(o_ref)


def _experts(xs, block_e, n_used, wg, bg, wu, bu, wd, bd):
    n_rows = xs.shape[0]
    nb = n_rows // MOE_BLOCK
    row_map = lambda i, be, nu: (jnp.maximum(jnp.minimum(i, nu[0] - 1), 0), 0, 0)
    w_map = lambda i, be, nu: (be[i], 0, 0)
    w_spec = pl.BlockSpec((1, D_MODEL, D_MODEL), w_map)
    b_spec = pl.BlockSpec((1, 1, D_MODEL), w_map)
    grid_spec = pltpu.PrefetchScalarGridSpec(
        num_scalar_prefetch=2,
        grid=(nb,),
        in_specs=[pl.BlockSpec((MOE_BLOCK, ROW_TILES, LANE), row_map), w_spec, b_spec, w_spec, b_spec, w_spec,
                  b_spec],
        out_specs=pl.BlockSpec((MOE_BLOCK, ROW_TILES, LANE), lambda i, be, nu: (i, 0, 0)),
        scratch_shapes=[pltpu.VMEM((D_MODEL, D_MODEL), BF16)] * 3 + [pltpu.VMEM((MOE_BLOCK, D_MODEL), BF16)],
    )
    return pl.pallas_call(
        _expert_kernel,
        grid_spec=grid_spec,
        out_shape=jax.ShapeDtypeStruct((n_rows, ROW_TILES, LANE), F32),
        compiler_params=pltpu.CompilerParams(dimension_semantics=("arbitrary",), vmem_limit_bytes=VMEM_LIMIT),
        name="experts",
    )(block_e, n_used, xs, wg, bg.reshape(N_EXPERTS, 1, D_MODEL), wu, bu.reshape(N_EXPERTS, 1, D_MODEL), wd,
      bd.reshape(N_EXPERTS, 1, D_MODEL))


def _combine_kernel(n_first, nt, d_ref, dn_ref, x1a_ref, x1b_ref, g_ref, nf_ref, eo_ref, ya_ref, yb_ref, buf, y_s, sem):
    i = pl.program_id(0)
    tl = ROUTE_TILE
    slot = lax.rem(i, 2)

    def gather(dest_ref, slot_, start):
        def body(j2, c):
            for u in range(DMA_UNROLL):
                j = j2 * DMA_UNROLL + u
                for k in range(TOP_K):
                    cp = _row_copy(eo_ref.at[dest_ref[k, j]], buf.at[slot_, k, j], sem.at[slot_])
                    if start:
                        cp.start(priority=(u + k) % 2)
                    else:
                        cp.wait()
            return c

        lax.fori_loop(0, tl // DMA_UNROLL, body, 0)

    @pl.when(i == 0)
    def _():
        gather(d_ref, 0, True)

    @pl.when(i + 1 < nt)
    def _():
        gather(dn_ref, 1 - slot, True)

    sel = (lax.broadcasted_iota(I32, (8, LANE), 0) == lax.broadcasted_iota(I32, (8, LANE), 1)).astype(F32)
    gt = _dot_tn(g_ref[...], sel, precision=HI)
    gather(d_ref, slot, False)
    for k in range(TOP_K):
        for s, slab in enumerate(_load_rows(buf.at[slot, k])):
            cols = slice(LANE * s, LANE * s + LANE)
            term = gt[:, k:k + 1] * slab
            y_s[:, cols] = term if k == 0 else y_s[:, cols] + term
    y = jnp.where(i < n_first, x1a_ref[...], x1b_ref[...]) + y_s[...]
    y = y * lax.rsqrt(jnp.mean(y * y, axis=-1, keepdims=True) + EPS) * nf_ref[...]

    @pl.when(i < n_first)
    def _():
        ya_ref[...] = y

    @pl.when(i >= n_first)
    def _():
        yb_ref[...] = y


def _combine(x1a, x1b, gates8, d3, eo3, nf):
    nt = (x1a.shape[0] + x1b.shape[0]) // ROUTE_TILE
    n_first = x1a.shape[0] // ROUTE_TILE
    map_a, map_b = _two_source_maps(n_first)
    row_specs = (pl.BlockSpec((ROUTE_TILE, D_MODEL), map_a), pl.BlockSpec((ROUTE_TILE, D_MODEL), map_b))
    grid_spec = pltpu.PrefetchScalarGridSpec(
        num_scalar_prefetch=0,
        grid=(nt,),
        in_specs=[pl.BlockSpec((None, TOP_K, ROUTE_TILE), lambda i: (i, 0, 0), memory_space=pltpu.SMEM),
                  pl.BlockSpec((None, TOP_K, ROUTE_TILE), lambda i: (jnp.minimum(i + 1, nt - 1), 0, 0),
                               memory_space=pltpu.SMEM),
                  *row_specs,
                  pl.BlockSpec((8, ROUTE_TILE), lambda i: (0, i)),
                  _const_spec(nf.shape),
                  pl.BlockSpec(memory_space=pl.ANY)],
        out_specs=row_specs,
        scratch_shapes=[pltpu.VMEM((2, TOP_K, ROUTE_TILE, ROW_TILES, LANE), F32),
                        pltpu.VMEM((ROUTE_TILE, D_MODEL), F32), pltpu.SemaphoreType.DMA((2,))],
    )
    return pl.pallas_call(
        functools.partial(_combine_kernel, n_first, nt),
        grid_spec=grid_spec,
        out_shape=(jax.ShapeDtypeStruct(x1a.shape, F32), jax.ShapeDtypeStruct(x1b.shape, F32)),
        compiler_params=pltpu.CompilerParams(dimension_semantics=("arbitrary",), vmem_limit_bytes=VMEM_LIMIT),
        name="combine",
    )(d3, d3, x1a, x1b, gates8, nf, eo3)


def _moe_and_norm(x1, xn2, lgt, wg, bg, wu, bu, wd, bd, nf):
    ta = x1[0].shape[0] + x1[1].shape[0]
    n_rows = (-(-(ta * TOP_K) // MOE_BLOCK) + N_EXPERTS) * MOE_BLOCK
    gates8, d3, cnt = _route(*lgt)
    counts = cnt[:, 0].astype(I32)
    padded = (counts + MOE_BLOCK - 1) // MOE_BLOCK * MOE_BLOCK
    pend = jnp.cumsum(padded)
    n_used = (pend[-1:] // MOE_BLOCK).astype(I32)
    block_row = jnp.arange(n_rows // MOE_BLOCK, dtype=I32) * MOE_BLOCK
    block_e = jnp.minimum(jnp.sum((pend[None, :] <= block_row[:, None]).astype(I32), axis=1), N_EXPERTS - 1)
    xs = _dispatch(*xn2, d3, pend.astype(I32), padded.astype(I32), n_used, n_rows)
    eo = _experts(xs, block_e, n_used, wg, bg, wu, bu, wd, bd)
    return _combine(*x1, gates8, d3, eo, nf)


def _prep_consts(norm1_w, w_in, conv_w, conv_b, dt_bias, A_log, D_skip, ssd_norm_w, hgrn_lower_bound, hgrn_norm_w,
                 w_out, norm2_w, w_router, b_router):
    row = lambda v: v.reshape(1, -1).astype(F32)
    pad_lane = lambda v: jnp.pad(v.reshape(1, -1).astype(F32), ((0, 0), (0, LANE - v.size)))
    win = jnp.concatenate([w_in[:, :2560], w_in[:, 2576:6672], w_in[:, 2560:2576],
                           jnp.zeros((D_MODEL, LANE - SSD_HEADS), w_in.dtype)], axis=1).astype(BF16)
    dskip = jnp.repeat(D_skip.astype(F32), SSD_P).reshape(1, -1)
    return (row(norm1_w), win, conv_w.astype(F32), row(conv_b), pad_lane(dt_bias), pad_lane(A_log), dskip,
            row(ssd_norm_w), hgrn_lower_bound.astype(F32), row(hgrn_norm_w), w_out.astype(BF16), row(norm2_w),
            w_router.T.astype(F32), b_router.reshape(-1, 1).astype(F32))


def kernel(x_prompt, x_sample, state_conv, state_ssm, state_hgrn, norm1_w, w_in, conv_w, conv_b, dt_bias, A_log, D_skip, ssd_norm_w, hgrn_lower_bound, hgrn_norm_w, w_out, norm2_w, w_router, b_router, w_gate, b_gate, w_up, b_up, w_down, b_down, norm_f_w):
    depth = w_in.shape[0]
    assert depth == 1
    bsz, seq, _ = x_prompt.shape
    nseq, tdec, _ = x_sample.shape
    consts = _prep_consts(norm1_w[0], w_in[0], conv_w[0], conv_b[0], dt_bias[0], A_log[0], D_skip[0], ssd_norm_w[0],
                          hgrn_lower_bound, hgrn_norm_w[0], w_out[0], norm2_w[0], w_router[0], b_router[0])
    n1, win, cw, cb, dtb, alog, dskip, snw, hlb, hnw, wout, n2, wrt, br = consts
    chunk_consts = (cw, cb, dtb, alog, dskip, snw, hlb, hnw)

    x1_p, xn2_p, lg_p, conv_p, ssm_p, hgrn_p = _prompt_mixer(x_prompt, consts)

    xs_flat = x_sample.reshape(nseq * tdec, D_MODEL)
    proj_s = _sample_inproj(xs_flat, n1, win).reshape(nseq, tdec, PW)
    mix_s, conv_s, ssm_s, hgrn_s = _sample_step(
        proj_s, state_conv[0], state_ssm[0].reshape(nseq, SSD_HEADS * SSD_P, SSD_N),
        state_hgrn[0].reshape(nseq, HGRN_HEADS * HGRN_D, HGRN_D), chunk_consts)
    x1_s, xn2_s, lg_s = _sample_out(mix_s.reshape(nseq * tdec, 2048), xs_flat, wout, n2, wrt, br)

    y_p, y_s = _moe_and_norm((x1_p, x1_s), (xn2_p, xn2_s), (lg_p, lg_s), w_gate[0], b_gate[0], w_up[0], b_up[0],
                             w_down[0], b_down[0], norm_f_w.reshape(1, -1).astype(F32))
    y_prompt = y_p.reshape(bsz, seq, D_MODEL)
    y_sample = y_s.reshape(nseq, tdec, D_MODEL)
    return (y_prompt, y_sample,
            conv_p[None], ssm_p.reshape(1, bsz, SSD_HEADS, SSD_P, SSD_N),
            hgrn_p.reshape(1, bsz, HGRN_HEADS, HGRN_D, HGRN_D),
            conv_s[None], ssm_s.reshape(1, nseq, SSD_HEADS, SSD_P, SSD_N),
            hgrn_s.reshape(1, nseq, HGRN_HEADS, HGRN_D, HGRN_D))
```

```python
import functools

import jax
import jax.numpy as jnp
from jax import lax
from jax.experimental import pallas as pl
from jax.experimental.pallas import tpu as pltpu

F32 = jnp.float32
BF16 = jnp.bfloat16
I32 = jnp.int32
HI = lax.Precision.HIGHEST

D_MODEL = 1024
SSD_HEADS = 16
SSD_P = 64
SSD_N = 128
CONV_DIM = 1536
CONV_W = 4
HGRN_HEADS = 8
HGRN_D = 128
N_EXPERTS = 32
TOP_K = 4
MOE_BLOCK = 512
SWIGLU_LIMIT = 7.0
SWIGLU_ALPHA = 1.702
EPS = 1e-6

Z0 = 0
X0 = 1024
Q0 = 2560
F0 = 3584
I0 = 4608
G0 = 5632
DT0 = 6656
PW = 6784
LANE = 128
HALO = 8
SSD_CHUNK = 128
HGRN_CHUNK = 16
MAX_SUB = SSD_CHUNK // HGRN_CHUNK
VMEM_LIMIT = 56 * 1024 * 1024


def _silu(x):
    return x * jax.nn.sigmoid(x)


def _softplus(x):
    return jnp.maximum(x, 0.0) + jnp.log1p(jnp.exp(-jnp.abs(x)))


def _dot(a, b):
    return jnp.dot(a, b, preferred_element_type=F32)


def _dot_nt(a, b, precision=None):
    return lax.dot_general(a, b, (((1,), (1,)), ((), ())), precision=precision, preferred_element_type=F32)


def _dot_tn(a, b, precision=None):
    return lax.dot_general(a, b, (((0,), (0,)), ((), ())), precision=precision, preferred_element_type=F32)


def _split3(x):
    hi = x.astype(BF16)
    r1 = x - hi.astype(F32)
    mid = r1.astype(BF16)
    lo = (r1 - mid.astype(F32)).astype(BF16)
    return hi, mid, lo


def _sel_dot(sel, x):
    return sum(_dot(sel, p) for p in _split3(x))


def _sel_dot_tn(x, sel):
    return sum(_dot_tn(p, sel) for p in _split3(x))


def _in_proj(x, n1_ref, win_ref, pj, r0):
    rows = x.shape[0]
    ms = jnp.mean(x * x, axis=-1, keepdims=True)
    h = (x * lax.rsqrt(ms + EPS) * n1_ref[...]).astype(BF16)
    for c0 in range(0, PW, 512):
        c1 = min(c0 + 512, PW)
        pj[r0:r0 + rows, c0:c1] = _dot(h, win_ref[:, c0:c1])


def _mix_chunk(pj, r0, L, nv, LH, ssm, hst, prm, scr, mix_s, m0):
    cw, cb, dtb, alog, dskip, snw, hlb, hnw = prm
    y_s, b_s, kk_s, r_s, q_s, qb_s, v_s, kend_s = scr

    def conv_cols(c0, c1):
        acc = cb[:, c0:c1]
        for k in range(CONV_W):
            acc = acc + pj[r0 - 3 + k:r0 - 3 + k + L, X0 + c0:X0 + c1] * cw[k:k + 1, c0:c1]
        return _silu(acc)

    rowi = lax.broadcasted_iota(I32, (L, L), 0)
    coli = lax.broadcasted_iota(I32, (L, L), 1)
    causal = coli <= rowi
    rid = lax.broadcasted_iota(I32, (L, LANE), 0)
    lane = lax.broadcasted_iota(I32, (L, LANE), 1)
    lo = lane < SSD_P

    dt = _softplus(pj[r0:r0 + L, DT0:DT0 + LANE] + dtb[...])
    if nv < L:
        dt = jnp.where(rid < nv, dt, 0.0)
    a = dt * (-jnp.exp(alog[...]))
    acum = _sel_dot(causal.astype(BF16), a)
    acum_t = _sel_dot_tn(a, (rowi <= coli).astype(BF16))
    dt_t = _sel_dot_tn(dt, (rowi == coli).astype(BF16))
    eac = jnp.exp(acum)
    last = acum[L - 1:L, :]
    elast = jnp.exp(last)
    tail = jnp.exp(last - acum) * dt

    bm = [conv_cols(1024 + 128 * g, 1152 + 128 * g).astype(BF16) for g in range(2)]
    cm = [conv_cols(1280 + 128 * g, 1408 + 128 * g).astype(BF16) for g in range(2)]
    gcb = [_dot_nt(cm[g], bm[g]) for g in range(2)]
    rowh = lax.broadcasted_iota(I32, (LANE, LANE), 0) < SSD_P

    for j in range(SSD_HEADS // 2):
        g = j // 4
        cols = slice(128 * j, 128 * j + 128)
        xs = conv_cols(128 * j, 128 * j + 128)
        xsb = xs.astype(BF16)
        ys = []
        for h in (2 * j, 2 * j + 1):
            seg = acum[:, h:h + 1] - acum_t[h:h + 1, :]
            dec = jnp.exp(jnp.where(causal, seg, -jnp.inf))
            mh = (gcb[g] * dec * dt_t[h:h + 1, :]).astype(BF16)
            ys.append(_dot(mh, xsb))
        y = jnp.where(lo, ys[0], ys[1])
        hp = ssm[128 * j:128 * j + 128, :]
        yoff = _dot_nt(cm[g], hp.astype(BF16))
        y = y + yoff * jnp.where(lo, eac[:, 2 * j:2 * j + 1], eac[:, 2 * j + 1:2 * j + 2])
        y = y + dskip[:, cols] * xs
        y_s[0:L, cols] = y
        xt = (xs * jnp.where(lo, tail[:, 2 * j:2 * j + 1], tail[:, 2 * j + 1:2 * j + 2])).astype(BF16)
        upd = _dot_tn(xt, bm[g])
        el = jnp.where(rowh, elast[:, 2 * j:2 * j + 1], elast[:, 2 * j + 1:2 * j + 2])
        ssm[128 * j:128 * j + 128, :] = el * hp + upd

    yz = y_s[0:L, :] * _silu(pj[r0:r0 + L, Z0:Z0 + 1024])
    for g in range(2):
        sl = slice(512 * g, 512 * g + 512)
        part = yz[:, sl]
        ms = jnp.mean(part * part, axis=-1, keepdims=True)
        mix_s[m0:m0 + L, sl] = (part * lax.rsqrt(ms + EPS) * snw[:, sl]).astype(mix_s.dtype)

    n_sub = L // LH
    AH = min(L, 2 * LH)
    n_att = L // AH
    r0v = hlb[0:1, :]
    r1v = hlb[1:2, :]
    mx = jnp.maximum(r0v, r1v)
    e0 = jnp.exp(r0v - mx)
    lb = e0 / (e0 + jnp.exp(r1v - mx))
    rid_w = lax.broadcasted_iota(I32, (L, 1024), 0)
    f = lb + (1.0 - lb) * jax.nn.sigmoid(pj[r0:r0 + L, F0:F0 + 1024])
    if nv < L:
        f = jnp.where(rid_w < nv, f, 1.0)
    kk = 1.0 - f
    blk = jnp.logical_and(causal, (rowi // LH) == (coli // LH)).astype(F32)
    b = _sel_dot(blk.astype(BF16), jnp.log(f))
    q = _silu(pj[r0:r0 + L, Q0:Q0 + 1024])
    base = jnp.zeros((1, 1024), F32)
    for i in range(n_sub):
        rows = slice(LH * i, LH * i + LH)
        bi = b[rows, :] + base
        b_s[rows, :] = bi
        qb_s[rows, :] = (q[rows, :] * jnp.exp(bi)).astype(BF16)
        base = bi[LH - 1:LH, :]
    for i in range(n_att):
        rows = slice(AH * i, AH * i + AH)
        mid = b_s[AH * i + AH // 2 - 1:AH * i + AH // 2, :]
        r_s[i:i + 1, :] = mid
        q_s[rows, :] = (q[rows, :] * jnp.exp(b_s[rows, :] - mid)).astype(BF16)
    kk_s[0:L, :] = kk
    v_s[0:L, :] = pj[r0:r0 + L, I0:I0 + 1024].astype(BF16)
    kend_s[0:L, :] = (kk * jnp.exp(base - b_s[0:L, :])).astype(BF16)
    ebl = jnp.exp(base)
    colq = lax.broadcasted_iota(I32, (AH, L), 1)
    rowq = lax.broadcasted_iota(I32, (AH, L), 0)

    for hh in range(HGRN_HEADS):
        c = slice(128 * hh, 128 * hh + 128)
        bh = b_s[0:L, c]
        kkh = kk_s[0:L, c]
        parts = []
        for i in range(n_att):
            n_keys = AH * i + AH
            ktil = (kkh[0:n_keys, :] * jnp.exp(r_s[i:i + 1, c] - bh[0:n_keys, :])).astype(BF16)
            if n_keys < L:
                ktil = jnp.concatenate([ktil, jnp.zeros((L - n_keys, LANE), BF16)], axis=0)
            att = _dot_nt(q_s[AH * i:AH * i + AH, c], ktil)
            parts.append(jnp.where(colq <= rowq + AH * i, att, 0.0).astype(BF16))
        att_all = parts[0] if n_att == 1 else jnp.concatenate(parts, axis=0)
        st = hst[c, :]
        oh = _dot(att_all, v_s[0:L, c]) + _dot_nt(qb_s[0:L, c], st.astype(BF16))
        hst[c, :] = ebl[:, c] * st + _dot_tn(v_s[0:L, c], kend_s[0:L, c])
        ms = jnp.mean(oh * oh, axis=-1, keepdims=True)
        on = oh * lax.rsqrt(ms + EPS) * hnw[:, c]
        mix_s[m0:m0 + L, 1024 + 128 * hh:1152 + 128 * hh] = (
            on * _silu(pj[r0:r0 + L, G0 + 128 * hh:G0 + 128 * hh + 128])).astype(mix_s.dtype)


def _chunk_scratch(rows):
    wide = lambda dt: pltpu.VMEM((rows, 1024), dt)
    return [wide(F32), wide(F32), wide(F32), pltpu.VMEM((MAX_SUB, 1024), F32), wide(BF16), wide(BF16), wide(BF16),
            wide(BF16)]


ROW_TILES = D_MODEL // LANE


def _store_rows(ref, val):
    for s in range(ROW_TILES):
        ref[:, s, :] = val[:, LANE * s:LANE * s + LANE]


def _load_rows(ref):
    planes = jnp.swapaxes(ref[...], 0, 1)
    return [planes[s] for s in range(ROW_TILES)]


def _out_part(mix, x, wout_ref, n2_ref, wrt_ref, br_ref):
    x1 = x + _dot(mix, wout_ref[...])
    ms = jnp.mean(x1 * x1, axis=-1, keepdims=True)
    xn2 = x1 * lax.rsqrt(ms + EPS) * n2_ref[...]
    lgt = _dot_nt(wrt_ref[...], xn2, precision=HI) + br_ref[...]
    return x1, xn2, lgt


def _prompt_kernel(tt, nt, x_ref, n1_ref, win_ref, cw, cb, dtb, alog, dskip, snw, hlb, hnw, wout_ref, n2_ref, wrt_ref,
                   br_ref, x1_ref, xn2_ref, lg_ref, conv_ref, ssm_ref, hgrn_ref,
                   pj, mix_s, hst, *scr):
    t = pl.program_id(1)

    @pl.when(t == 0)
    def _():
        pj[0:HALO, :] = jnp.zeros((HALO, PW), F32)
        ssm_ref[0] = jnp.zeros((SSD_HEADS * SSD_P, SSD_N), F32)
        hst[...] = jnp.zeros_like(hst)

    x = x_ref[0]
    _in_proj(x, n1_ref, win_ref, pj, HALO)
    prm = (cw, cb, dtb, alog, dskip, snw, hlb, hnw)
    n_scr = len(scr) // (tt // SSD_CHUNK)
    for c in range(tt // SSD_CHUNK):
        _mix_chunk(pj, HALO + SSD_CHUNK * c, SSD_CHUNK, SSD_CHUNK, HGRN_CHUNK, ssm_ref.at[0], hst, prm,
                   scr[n_scr * c:n_scr * c + n_scr], mix_s, SSD_CHUNK * c)
    x1, xn2, lgt = _out_part(mix_s[...], x, wout_ref, n2_ref, wrt_ref, br_ref)
    x1_ref[...] = x1
    _store_rows(xn2_ref, xn2)
    lg_ref[...] = lgt

    @pl.when(t == nt - 1)
    def _():
        conv_ref[0] = pj[HALO + tt - 3:HALO + tt, X0:X0 + CONV_DIM]
        for hh in range(HGRN_HEADS):
            c = slice(128 * hh, 128 * hh + 128)
            hgrn_ref[0, c, :] = hst[c, :].T

    pj[0:HALO, X0:X0 + CONV_DIM] = pj[tt:tt + HALO, X0:X0 + CONV_DIM]


def _const_spec(shape):
    nd = len(shape)
    return pl.BlockSpec(shape, lambda *_: (0,) * nd, pipeline_mode=pl.Buffered(1))


def _prompt_mixer(x, consts, tt=256):
    bsz, seq, _ = x.shape
    nt = seq // tt
    ta = bsz * seq
    const_specs = [_const_spec(c.shape) for c in consts]
    out_shape = (
        jax.ShapeDtypeStruct((ta, D_MODEL), F32),
        jax.ShapeDtypeStruct((ta, ROW_TILES, LANE), F32),
        jax.ShapeDtypeStruct((N_EXPERTS, ta), F32),
        jax.ShapeDtypeStruct((bsz, CONV_W - 1, CONV_DIM), F32),
        jax.ShapeDtypeStruct((bsz, SSD_HEADS * SSD_P, SSD_N), F32),
        jax.ShapeDtypeStruct((bsz, HGRN_HEADS * HGRN_D, HGRN_D), F32),
    )
    out_specs = (
        pl.BlockSpec((tt, D_MODEL), lambda b, t: (b * nt + t, 0)),
        pl.BlockSpec((tt, ROW_TILES, LANE), lambda b, t: (b * nt + t, 0, 0)),
        pl.BlockSpec((N_EXPERTS, tt), lambda b, t: (0, b * nt + t)),
        pl.BlockSpec((1, CONV_W - 1, CONV_DIM), lambda b, t: (b, 0, 0)),
        pl.BlockSpec((1, SSD_HEADS * SSD_P, SSD_N), lambda b, t: (b, 0, 0)),
        pl.BlockSpec((1, HGRN_HEADS * HGRN_D, HGRN_D), lambda b, t: (b, 0, 0)),
    )
    scratch = [
        pltpu.VMEM((HALO + tt, PW), F32),
        pltpu.VMEM((tt, 2048), BF16),
        pltpu.VMEM((HGRN_HEADS * HGRN_D, HGRN_D), F32),
    ] + _chunk_scratch(SSD_CHUNK) * (tt // SSD_CHUNK)
    return pl.pallas_call(
        functools.partial(_prompt_kernel, tt, nt),
        grid=(bsz, nt),
        in_specs=[pl.BlockSpec((1, tt, D_MODEL), lambda b, t: (b, t, 0))] + const_specs,
        out_specs=out_specs,
        out_shape=out_shape,
        scratch_shapes=scratch,
        compiler_params=pltpu.CompilerParams(dimension_semantics=("arbitrary", "arbitrary"),
                                             vmem_limit_bytes=VMEM_LIMIT),
        name="prompt_mixer",
    )(x, *consts)


def _sample_inproj_kernel(x_ref, n1_ref, win_ref, o_ref):
    _in_proj(x_ref[...], n1_ref, win_ref, o_ref, 0)


def _sample_inproj(x, n1, win, rows=256):
    n = x.shape[0]
    return pl.pallas_call(
        _sample_inproj_kernel,
        grid=(n // rows,),
        in_specs=[pl.BlockSpec((rows, D_MODEL), lambda i: (i, 0)), _const_spec(n1.shape), _const_spec(win.shape)],
        out_specs=pl.BlockSpec((rows, PW), lambda i: (i, 0)),
        out_shape=jax.ShapeDtypeStruct((n, PW), F32),
        compiler_params=pltpu.CompilerParams(dimension_semantics=("arbitrary",), vmem_limit_bytes=VMEM_LIMIT),
        name="sample_inproj",
    )(x, n1, win)


SAMPLE_L = 16


def _sample_step_kernel(tdec, pj_ref, cin_ref, sin_ref, hin_ref, cw, cb, dtb, alog, dskip, snw, hlb, hnw,
                        mix_ref, cout_ref, sout_ref, hout_ref,
                        pj, mix_s, hst, *scr):
    pj[...] = jnp.zeros_like(pj)
    pj[HALO - 3:HALO, X0:X0 + CONV_DIM] = cin_ref[0]
    pj[HALO:HALO + tdec, :] = pj_ref[0]
    sout_ref[0] = sin_ref[0]
    for hh in range(HGRN_HEADS):
        c = slice(128 * hh, 128 * hh + 128)
        hst[c, :] = hin_ref[0, c, :].T
    prm = (cw, cb, dtb, alog, dskip, snw, hlb, hnw)
    _mix_chunk(pj, HALO, SAMPLE_L, tdec, SAMPLE_L, sout_ref.at[0], hst, prm, scr, mix_s, 0)
    mix_ref[0] = mix_s[0:tdec, :]
    cout_ref[0] = pj[HALO + tdec - 3:HALO + tdec, X0:X0 + CONV_DIM]
    for hh in range(HGRN_HEADS):
        c = slice(128 * hh, 128 * hh + 128)
        hout_ref[0, c, :] = hst[c, :].T


def _sample_step(proj, conv_in, ssm_in, hgrn_in, consts):
    nseq, tdec, _ = proj.shape
    seq_spec = lambda shape: pl.BlockSpec((1,) + shape, lambda i: (i,) + (0,) * len(shape))
    scratch = [
        pltpu.VMEM((HALO + SAMPLE_L, PW), F32),
        pltpu.VMEM((SAMPLE_L, 2048), F32),
        pltpu.VMEM((HGRN_HEADS * HGRN_D, HGRN_D), F32),
    ] + _chunk_scratch(SAMPLE_L)
    return pl.pallas_call(
        functools.partial(_sample_step_kernel, tdec),
        grid=(nseq,),
        in_specs=[seq_spec((tdec, PW)), seq_spec((CONV_W - 1, CONV_DIM)), seq_spec((SSD_HEADS * SSD_P, SSD_N)),
                  seq_spec((HGRN_HEADS * HGRN_D, HGRN_D))] + [_const_spec(c.shape) for c in consts],
        out_specs=(seq_spec((tdec, 2048)), seq_spec((CONV_W - 1, CONV_DIM)), seq_spec((SSD_HEADS * SSD_P, SSD_N)),
                   seq_spec((HGRN_HEADS * HGRN_D, HGRN_D))),
        out_shape=(jax.ShapeDtypeStruct((nseq, tdec, 2048), F32),
                   jax.ShapeDtypeStruct((nseq, CONV_W - 1, CONV_DIM), F32),
                   jax.ShapeDtypeStruct((nseq, SSD_HEADS * SSD_P, SSD_N), F32),
                   jax.ShapeDtypeStruct((nseq, HGRN_HEADS * HGRN_D, HGRN_D), F32)),
        scratch_shapes=scratch,
        compiler_params=pltpu.CompilerParams(dimension_semantics=("arbitrary",), vmem_limit_bytes=VMEM_LIMIT),
        name="sample_step",
    )(proj, conv_in, ssm_in, hgrn_in, *consts)


def _sample_out_kernel(mix_ref, x_ref, wout_ref, n2_ref, wrt_ref, br_ref, x1_ref, xn2_ref, lg_ref):
    x1, xn2, lgt = _out_part(mix_ref[...].astype(BF16), x_ref[...], wout_ref, n2_ref, wrt_ref, br_ref)
    x1_ref[...] = x1
    _store_rows(xn2_ref, xn2)
    lg_ref[...] = lgt


def _sample_out(mix, x, wout, n2, wrt, br):
    n = x.shape[0]
    return pl.pallas_call(
        _sample_out_kernel,
        out_shape=(jax.ShapeDtypeStruct((n, D_MODEL), F32), jax.ShapeDtypeStruct((n, ROW_TILES, LANE), F32),
                   jax.ShapeDtypeStruct((N_EXPERTS, n), F32)),
        compiler_params=pltpu.CompilerParams(vmem_limit_bytes=VMEM_LIMIT),
        name="sample_out",
    )(mix, x, wout, n2, wrt, br)


ROUTE_TILE = 256


def _route_kernel(n_first, lga_ref, lgb_ref, g_ref, d_ref, cnt_ref, cnt, carry, pstart):
    ph = pl.program_id(0)
    i = pl.program_id(1)
    tl = ROUTE_TILE
    l = jnp.where(i < n_first, lga_ref[...], lgb_ref[...])
    eid = lax.broadcasted_iota(I32, (N_EXPERTS, tl), 0)
    hots, vals = [], []
    for _ in range(TOP_K):
        m = jnp.max(l, axis=0, keepdims=True)
        idx = jnp.min(jnp.where(l == m, eid, N_EXPERTS), axis=0, keepdims=True)
        hot = eid == idx
        hots.append(hot)
        vals.append(m)
        l = jnp.where(hot, -jnp.inf, l)
    ind = hots[0].astype(F32)
    for hot in hots[1:]:
        ind = ind + hot.astype(F32)
    tile_cnt = jnp.sum(ind, axis=1, keepdims=True)

    @pl.when(jnp.logical_and(ph == 0, i == 0))
    def _():
        cnt[...] = jnp.zeros_like(cnt)

    @pl.when(ph == 0)
    def _():
        cnt[...] = cnt[...] + tile_cnt

    @pl.when(jnp.logical_and(ph == 1, i == 0))
    def _():
        padded = jnp.floor((cnt[...] + (MOE_BLOCK - 1)) * (1.0 / MOE_BLOCK)) * MOE_BLOCK
        r = lax.broadcasted_iota(I32, (N_EXPERTS, N_EXPERTS), 0)
        c = lax.broadcasted_iota(I32, (N_EXPERTS, N_EXPERTS), 1)
        pend = jnp.dot((c <= r).astype(F32), padded, precision=HI, preferred_element_type=F32)
        pstart[...] = pend - padded
        carry[...] = jnp.zeros_like(carry)
        cnt_ref[...] = cnt[...]

    @pl.when(ph == 1)
    def _():
        rr = lax.broadcasted_iota(I32, (tl, tl), 0)
        cc = lax.broadcasted_iota(I32, (tl, tl), 1)
        before = _dot(ind.astype(BF16), (rr < cc).astype(BF16))
        base = before + carry[:, 0:1] + pstart[:, 0:1]
        den = jnp.exp(vals[0] - vals[0])
        for k in range(1, TOP_K):
            den = den + jnp.exp(vals[k] - vals[0])
        g_ref[...] = jnp.zeros_like(g_ref)
        for k in range(TOP_K):
            g_ref[k:k + 1, :] = jnp.exp(vals[k] - vals[0]) / den
            d_ref[k:k + 1, :] = jnp.sum(jnp.where(hots[k], base, 0.0), axis=0, keepdims=True).astype(I32)
        carry[...] = carry[...] + tile_cnt


def _two_source_maps(n_first, trailing=1):
    pad = (0,) * trailing
    return (lambda *idx: (jnp.minimum(idx[-1], n_first - 1),) + pad,
            lambda *idx: (jnp.maximum(idx[-1] - n_first, 0),) + pad)


def _route(lg_a, lg_b):
    ta = lg_a.shape[1] + lg_b.shape[1]
    nt = ta // ROUTE_TILE
    n_first = lg_a.shape[1] // ROUTE_TILE
    assert n_first * ROUTE_TILE == lg_a.shape[1] and nt * ROUTE_TILE == ta
    return pl.pallas_call(
        functools.partial(_route_kernel, n_first),
        grid=(2, nt),
        in_specs=[pl.BlockSpec((N_EXPERTS, ROUTE_TILE), lambda p, i: (0, jnp.minimum(i, n_first - 1))),
                  pl.BlockSpec((N_EXPERTS, ROUTE_TILE), lambda p, i: (0, jnp.maximum(i - n_first, 0)))],
        out_specs=(pl.BlockSpec((8, ROUTE_TILE), lambda p, i: (0, i * p)),
                   pl.BlockSpec((None, TOP_K, ROUTE_TILE), lambda p, i: (i * p, 0, 0)),
                   pl.BlockSpec((N_EXPERTS, LANE), lambda p, i: (0, 0))),
        out_shape=(jax.ShapeDtypeStruct((8, ta), F32), jax.ShapeDtypeStruct((nt, TOP_K, ROUTE_TILE), I32),
                   jax.ShapeDtypeStruct((N_EXPERTS, LANE), F32)),
        scratch_shapes=[pltpu.VMEM((N_EXPERTS, LANE), F32)] * 3,
        compiler_params=pltpu.CompilerParams(dimension_semantics=("arbitrary", "arbitrary")),
        name="route",
    )(lg_a, lg_b)


DMA_UNROLL = 8


def _row_copy(src, dst, sem):
    return pltpu.make_async_copy(src, dst, sem)


def _dispatch_kernel(n_first, pend_ref, padded_ref, nu_ref, d_ref, xa_ref, xb_ref, o_ref, zeros, sem):
    i = pl.program_id(0)
    tl = ROUTE_TILE
    nb = o_ref.shape[0] // MOE_BLOCK

    @pl.when(i == 0)
    def _():
        zeros[...] = jnp.zeros_like(zeros)

        def fills(e):
            return ((padded_ref[e] > 0, pend_ref[e] - MOE_BLOCK),
                    (nu_ref[0] + e < nb, (nu_ref[0] + e) * MOE_BLOCK))

        for e in range(N_EXPERTS):
            for cond, row in fills(e):
                @pl.when(cond)
                def _(row=row):
                    _row_copy(zeros, o_ref.at[pl.ds(row, MOE_BLOCK)], sem).start()
        for e in range(N_EXPERTS):
            for cond, row in fills(e):
                @pl.when(cond)
                def _(row=row):
                    _row_copy(zeros, o_ref.at[pl.ds(row, MOE_BLOCK)], sem).wait()

    def scatter_rows(x_ref):
        def start(j2, c):
            for u in range(DMA_UNROLL):
                j = j2 * DMA_UNROLL + u
                for k in range(TOP_K):
                    _row_copy(x_ref.at[j], o_ref.at[d_ref[k, j]], sem).start(priority=(u + k) % 2)
            return c

        lax.fori_loop(0, tl // DMA_UNROLL, start, 0)

        def wait(j2, c):
            for u in range(DMA_UNROLL):
                j = j2 * DMA_UNROLL + u
                for k in range(TOP_K):
                    _row_copy(x_ref.at[j], o_ref.at[d_ref[k, j]], sem).wait()
            return c

        lax.fori_loop(0, tl // DMA_UNROLL, wait, 0)

    @pl.when(i < n_first)
    def _():
        scatter_rows(xa_ref)

    @pl.when(i >= n_first)
    def _():
        scatter_rows(xb_ref)


def _dispatch(xa, xb, d3, pend, padded, n_used, n_rows):
    nt = (xa.shape[0] + xb.shape[0]) // ROUTE_TILE
    n_first = xa.shape[0] // ROUTE_TILE
    map_a, map_b = _two_source_maps(n_first, trailing=2)
    grid_spec = pltpu.PrefetchScalarGridSpec(
        num_scalar_prefetch=3,
        grid=(nt,),
        in_specs=[pl.BlockSpec((None, TOP_K, ROUTE_TILE), lambda i, *_: (i, 0, 0), memory_space=pltpu.SMEM),
                  pl.BlockSpec((ROUTE_TILE, ROW_TILES, LANE), lambda i, *_: map_a(i)),
                  pl.BlockSpec((ROUTE_TILE, ROW_TILES, LANE), lambda i, *_: map_b(i))],
        out_specs=pl.BlockSpec(memory_space=pl.ANY),
        scratch_shapes=[pltpu.VMEM((MOE_BLOCK, ROW_TILES, LANE), F32), pltpu.SemaphoreType.DMA],
    )
    return pl.pallas_call(
        functools.partial(_dispatch_kernel, n_first),
        grid_spec=grid_spec,
        out_shape=jax.ShapeDtypeStruct((n_rows, ROW_TILES, LANE), F32),
        compiler_params=pltpu.CompilerParams(dimension_semantics=("arbitrary",)),
        name="dispatch",
    )(pend, padded, n_used, d3, xa, xb)


def _expert_kernel(be_ref, nu_ref, x_ref, wg_ref, bg_ref, wu_ref, bu_ref, wd_ref, bd_ref, o_ref, wg_s, wu_s, wd_s,
                   lhs_s):
    i = pl.program_id(0)
    prev = be_ref[jnp.maximum(i - 1, 0)]
    fresh = jnp.logical_or(i == 0, be_ref[i] != prev)

    @pl.when(jnp.logical_and(i < nu_ref[0], fresh))
    def _():
        wg_s[...] = wg_ref[0].astype(BF16)
        wu_s[...] = wu_ref[0].astype(BF16)
        wd_s[...] = wd_ref[0].astype(BF16)

    @pl.when(i < nu_ref[0])
    def _():
        for s, slab in enumerate(_load_rows(x_ref)):
            lhs_s[:, LANE * s:LANE * s + LANE] = slab.astype(BF16)
        x = lhs_s[...]
        g = _dot(x, wg_s[...]) + bg_ref[0]
        u = _dot(x, wu_s[...]) + bu_ref[0]
        g = jnp.minimum(g, SWIGLU_LIMIT)
        u = jnp.clip(u, -SWIGLU_LIMIT, SWIGLU_LIMIT)
        act = g * jax.nn.sigmoid(SWIGLU_ALPHA * g) * (u + 1.0)
        _store_rows(o_ref, _dot(act.astype(BF16), wd_s[...]) + bd_ref[0])

    @pl.when(i >= nu_ref[0])
    def _():
        o_ref[...] = jnp.zeros_like(o_ref)


def _experts(xs, block_e, n_used, wg, bg, wu, bu, wd, bd):
    n_rows = xs.shape[0]
    nb = n_rows // MOE_BLOCK
    row_map = lambda i, be, nu: (jnp.maximum(jnp.minimum(i, nu[0] - 1), 0), 0, 0)
    w_map = lambda i, be, nu: (be[i], 0, 0)
    w_spec = pl.BlockSpec((1, D_MODEL, D_MODEL), w_map)
    b_spec = pl.BlockSpec((1, 1, D_MODEL), w_map)
    grid_spec = pltpu.PrefetchScalarGridSpec(
        num_scalar_prefetch=2,
        grid=(nb,),
        in_specs=[pl.BlockSpec((MOE_BLOCK, ROW_TILES, LANE), row_map), w_spec, b_spec, w_spec, b_spec, w_spec,
                  b_spec],
        out_specs=pl.BlockSpec((MOE_BLOCK, ROW_TILES, LANE), lambda i, be, nu: (i, 0, 0)),
        scratch_shapes=[pltpu.VMEM((D_MODEL, D_MODEL), BF16)] * 3 + [pltpu.VMEM((MOE_BLOCK, D_MODEL), BF16)],
    )
    return pl.pallas_call(
        _expert_kernel,
        grid_spec=grid_spec,
        out_shape=jax.ShapeDtypeStruct((n_rows, ROW_TILES, LANE), F32),
        compiler_params=pltpu.CompilerParams(dimension_semantics=("arbitrary",), vmem_limit_bytes=VMEM_LIMIT),
        name="experts",
    )(block_e, n_used, xs, wg, bg.reshape(N_EXPERTS, 1, D_MODEL), wu, bu.reshape(N_EXPERTS, 1, D_MODEL), wd,
      bd.reshape(N_EXPERTS, 1, D_MODEL))


def _combine_kernel(n_first, d_ref, x1a_ref, x1b_ref, g_ref, nf_ref, eo_ref, ya_ref, yb_ref, buf, y_s, sem):
    i = pl.program_id(0)
    tl = ROUTE_TILE

    def gather(start):
        def body(j2, c):
            for u in range(DMA_UNROLL):
                j = j2 * DMA_UNROLL + u
                for k in range(TOP_K):
                    cp = _row_copy(eo_ref.at[d_ref[k, j]], buf.at[k, j], sem)
                    if start:
                        cp.start(priority=(u + k) % 2)
                    else:
                        cp.wait()
            return c

        lax.fori_loop(0, tl // DMA_UNROLL, body, 0)

    gather(True)
    sel = (lax.broadcasted_iota(I32, (8, LANE), 0) == lax.broadcasted_iota(I32, (8, LANE), 1)).astype(F32)
    gt = _dot_tn(g_ref[...], sel, precision=HI)
    gather(False)
    for k in range(TOP_K):
        for s, slab in enumerate(_load_rows(buf.at[k])):
            cols = slice(LANE * s, LANE * s + LANE)
            term = gt[:, k:k + 1] * slab
            y_s[:, cols] = term if k == 0 else y_s[:, cols] + term
    y = jnp.where(i < n_first, x1a_ref[...], x1b_ref[...]) + y_s[...]
    y = y * lax.rsqrt(jnp.mean(y * y, axis=-1, keepdims=True) + EPS) * nf_ref[...]

    @pl.when(i < n_first)
    def _():
        ya_ref[...] = y

    @pl.when(i >= n_first)
    def _():
        yb_ref[...] = y


def _combine(x1a, x1b, gates8, d3, eo3, nf):
    nt = (x1a.shape[0] + x1b.shape[0]) // ROUTE_TILE
    n_first = x1a.shape[0] // ROUTE_TILE
    map_a, map_b = _two_source_maps(n_first)
    row_specs = (pl.BlockSpec((ROUTE_TILE, D_MODEL), map_a), pl.BlockSpec((ROUTE_TILE, D_MODEL), map_b))
    grid_spec = pltpu.PrefetchScalarGridSpec(
        num_scalar_prefetch=0,
        grid=(nt,),
        in_specs=[pl.BlockSpec((None, TOP_K, ROUTE_TILE), lambda i: (i, 0, 0), memory_space=pltpu.SMEM),
                  *row_specs,
                  pl.BlockSpec((8, ROUTE_TILE), lambda i: (0, i)),
                  _const_spec(nf.shape),
                  pl.BlockSpec(memory_space=pl.ANY)],
        out_specs=row_specs,
        scratch_shapes=[pltpu.VMEM((TOP_K, ROUTE_TILE, ROW_TILES, LANE), F32),
                        pltpu.VMEM((ROUTE_TILE, D_MODEL), F32), pltpu.SemaphoreType.DMA],
    )
    return pl.pallas_call(
        functools.partial(_combine_kernel, n_first),
        grid_spec=grid_spec,
        out_shape=(jax.ShapeDtypeStruct(x1a.shape, F32), jax.ShapeDtypeStruct(x1b.shape, F32)),
        compiler_params=pltpu.CompilerParams(dimension_semantics=("arbitrary",), vmem_limit_bytes=VMEM_LIMIT),
        name="combine",
    )(d3, x1a, x1b, gates8, nf, eo3)


def _moe_and_norm(x1, xn2, lgt, wg, bg, wu, bu, wd, bd, nf):
    ta = x1[0].shape[0] + x1[1].shape[0]
    n_rows = (-(-(ta * TOP_K) // MOE_BLOCK) + N_EXPERTS) * MOE_BLOCK
    gates8, d3, cnt = _route(*lgt)
    counts = cnt[:, 0].astype(I32)
    padded = (counts + MOE_BLOCK - 1) // MOE_BLOCK * MOE_BLOCK
    pend = jnp.cumsum(padded)
    n_used = (pend[-1:] // MOE_BLOCK).astype(I32)
    block_row = jnp.arange(n_rows // MOE_BLOCK, dtype=I32) * MOE_BLOCK
    block_e = jnp.minimum(jnp.sum((pend[None, :] <= block_row[:, None]).astype(I32), axis=1), N_EXPERTS - 1)
    xs = _dispatch(*xn2, d3, pend.astype(I32), padded.astype(I32), n_used, n_rows)
    eo = _experts(xs, block_e, n_used, wg, bg, wu, bu, wd, bd)
    return _combine(*x1, gates8, d3, eo, nf)


def _prep_consts(norm1_w, w_in, conv_w, conv_b, dt_bias, A_log, D_skip, ssd_norm_w, hgrn_lower_bound, hgrn_norm_w,
                 w_out, norm2_w, w_router, b_router):
    row = lambda v: v.reshape(1, -1).astype(F32)
    pad_lane = lambda v: jnp.pad(v.reshape(1, -1).astype(F32), ((0, 0), (0, LANE - v.size)))
    wb = w_in.astype(BF16)
    win = jnp.concatenate([wb[:, :2560], wb[:, 2576:6672], wb[:, 2560:2576],
                           jnp.zeros((D_MODEL, LANE - SSD_HEADS), BF16)], axis=1)
    dskip = jnp.repeat(D_skip.astype(F32), SSD_P).reshape(1, -1)
    return (row(norm1_w), win, conv_w.astype(F32), row(conv_b), pad_lane(dt_bias), pad_lane(A_log), dskip,
            row(ssd_norm_w), hgrn_lower_bound.astype(F32), row(hgrn_norm_w), w_out.astype(BF16), row(norm2_w),
            w_router.T.astype(F32), b_router.reshape(-1, 1).astype(F32))


def kernel(x_prompt, x_sample, state_conv, state_ssm, state_hgrn, norm1_w, w_in, conv_w, conv_b, dt_bias, A_log, D_skip, ssd_norm_w, hgrn_lower_bound, hgrn_norm_w, w_out, norm2_w, w_router, b_router, w_gate, b_gate, w_up, b_up, w_down, b_down, norm_f_w):
    depth = w_in.shape[0]
    assert depth == 1
    bsz, seq, _ = x_prompt.shape
    nseq, tdec, _ = x_sample.shape
    consts = _prep_consts(norm1_w[0], w_in[0], conv_w[0], conv_b[0], dt_bias[0], A_log[0], D_skip[0], ssd_norm_w[0],
                          hgrn_lower_bound, hgrn_norm_w[0], w_out[0], norm2_w[0], w_router[0], b_router[0])
    n1, win, cw, cb, dtb, alog, dskip, snw, hlb, hnw, wout, n2, wrt, br = consts
    chunk_consts = (cw, cb, dtb, alog, dskip, snw, hlb, hnw)

    x1_p, xn2_p, lg_p, conv_p, ssm_p, hgrn_p = _prompt_mixer(x_prompt, consts)

    xs_flat = x_sample.reshape(nseq * tdec, D_MODEL)
    proj_s = _sample_inproj(xs_flat, n1, win).reshape(nseq, tdec, PW)
    mix_s, conv_s, ssm_s, hgrn_s = _sample_step(
        proj_s, state_conv[0], state_ssm[0].reshape(nseq, SSD_HEADS * SSD_P, SSD_N),
        state_hgrn[0].reshape(nseq, HGRN_HEADS * HGRN_D, HGRN_D), chunk_consts)
    x1_s, xn2_s, lg_s = _sample_out(mix_s.reshape(nseq * tdec, 2048), xs_flat, wout, n2, wrt, br)

    y_p, y_s = _moe_and_norm((x1_p, x1_s), (xn2_p, xn2_s), (lg_p, lg_s), w_gate[0], b_gate[0], w_up[0], b_up[0],
                             w_down[0], b_down[0], norm_f_w.reshape(1, -1).astype(F32))
    y_prompt = y_p.reshape(bsz, seq, D_MODEL)
    y_sample = y_s.reshape(nseq, tdec, D_MODEL)
    return (y_prompt, y_sample,
            conv_p[None], ssm_p.reshape(1, bsz, SSD_HEADS, SSD_P, SSD_N),
            hgrn_p.reshape(1, bsz, HGRN_HEADS, HGRN_D, HGRN_D),
            conv_s[None], ssm_s.reshape(1, nseq, SSD_HEADS, SSD_P, SSD_N),
            hgrn_s.reshape(1, nseq, HGRN_HEADS, HGRN_D, HGRN_D))
```

```python
import functools

import jax
import jax.numpy as jnp
from jax import lax
from jax.experimental import pallas as pl
from jax.experimental.pallas import tpu as pltpu

F32 = jnp.float32
BF16 = jnp.bfloat16
I32 = jnp.int32
HI = lax.Precision.HIGHEST

D_MODEL = 1024
SSD_HEADS = 16
SSD_P = 64
SSD_N = 128
CONV_DIM = 1536
CONV_W = 4
HGRN_HEADS = 8
HGRN_D = 128
N_EXPERTS = 32
TOP_K = 4
MOE_BLOCK = 512
SWIGLU_LIMIT = 7.0
SWIGLU_ALPHA = 1.702
EPS = 1e-6

Z0 = 0
X0 = 1024
Q0 = 2560
F0 = 3584
I0 = 4608
G0 = 5632
DT0 = 6656
PW = 6784
LANE = 128
HALO = 8
SSD_CHUNK = 128
HGRN_CHUNK = 16
MAX_SUB = SSD_CHUNK // HGRN_CHUNK
VMEM_LIMIT = 56 * 1024 * 1024


def _silu(x):
    return x * jax.nn.sigmoid(x)


def _softplus(x):
    return jnp.maximum(x, 0.0) + jnp.log1p(jnp.exp(-jnp.abs(x)))


def _dot(a, b):
    return jnp.dot(a, b, preferred_element_type=F32)


def _dot_nt(a, b, precision=None):
    return lax.dot_general(a, b, (((1,), (1,)), ((), ())), precision=precision, preferred_element_type=F32)


def _dot_tn(a, b, precision=None):
    return lax.dot_general(a, b, (((0,), (0,)), ((), ())), precision=precision, preferred_element_type=F32)


def _split3(x):
    hi = x.astype(BF16)
    r1 = x - hi.astype(F32)
    mid = r1.astype(BF16)
    lo = (r1 - mid.astype(F32)).astype(BF16)
    return hi, mid, lo


def _sel_dot(sel, x):
    return sum(_dot(sel, p) for p in _split3(x))


def _sel_dot_tn(x, sel):
    return sum(_dot_tn(p, sel) for p in _split3(x))


def _in_proj(x, n1_ref, win_ref, pj, r0):
    rows = x.shape[0]
    ms = jnp.mean(x * x, axis=-1, keepdims=True)
    h = (x * lax.rsqrt(ms + EPS) * n1_ref[...]).astype(BF16)
    for c0 in range(0, PW, 512):
        c1 = min(c0 + 512, PW)
        pj[r0:r0 + rows, c0:c1] = _dot(h, win_ref[:, c0:c1])


def _mix_chunk(pj, r0, L, nv, LH, ssm, hst, prm, scr, mix_s, m0):
    cw, cb, dtb, alog, dskip, snw, hlb, hnw = prm
    y_s, b_s, kk_s, r_s, q_s, qb_s, v_s, kend_s = scr

    def conv_cols(c0, c1):
        acc = cb[:, c0:c1]
        for k in range(CONV_W):
            acc = acc + pj[r0 - 3 + k:r0 - 3 + k + L, X0 + c0:X0 + c1] * cw[k:k + 1, c0:c1]
        return _silu(acc)

    rowi = lax.broadcasted_iota(I32, (L, L), 0)
    coli = lax.broadcasted_iota(I32, (L, L), 1)
    causal = coli <= rowi
    rid = lax.broadcasted_iota(I32, (L, LANE), 0)
    lane = lax.broadcasted_iota(I32, (L, LANE), 1)
    lo = lane < SSD_P

    dt = _softplus(pj[r0:r0 + L, DT0:DT0 + LANE] + dtb[...])
    if nv < L:
        dt = jnp.where(rid < nv, dt, 0.0)
    a = dt * (-jnp.exp(alog[...]))
    acum = _sel_dot(causal.astype(BF16), a)
    acum_t = _sel_dot_tn(a, (rowi <= coli).astype(BF16))
    dt_t = _sel_dot_tn(dt, (rowi == coli).astype(BF16))
    eac = jnp.exp(acum)
    last = acum[L - 1:L, :]
    elast = jnp.exp(last)
    tail = jnp.exp(last - acum) * dt

    bm = [conv_cols(1024 + 128 * g, 1152 + 128 * g).astype(BF16) for g in range(2)]
    cm = [conv_cols(1280 + 128 * g, 1408 + 128 * g).astype(BF16) for g in range(2)]
    gcb = [_dot_nt(cm[g], bm[g]) for g in range(2)]
    rowh = lax.broadcasted_iota(I32, (LANE, LANE), 0) < SSD_P

    for j in range(SSD_HEADS // 2):
        g = j // 4
        cols = slice(128 * j, 128 * j + 128)
        xs = conv_cols(128 * j, 128 * j + 128)
        xsb = xs.astype(BF16)
        ys = []
        for h in (2 * j, 2 * j + 1):
            seg = acum[:, h:h + 1] - acum_t[h:h + 1, :]
            dec = jnp.exp(jnp.where(causal, seg, -jnp.inf))
            mh = (gcb[g] * dec * dt_t[h:h + 1, :]).astype(BF16)
            ys.append(_dot(mh, xsb))
        y = jnp.where(lo, ys[0], ys[1])
        hp = ssm[128 * j:128 * j + 128, :]
        yoff = _dot_nt(cm[g], hp.astype(BF16))
        y = y + yoff * jnp.where(lo, eac[:, 2 * j:2 * j + 1], eac[:, 2 * j + 1:2 * j + 2])
        y = y + dskip[:, cols] * xs
        y_s[0:L, cols] = y
        xt = (xs * jnp.where(lo, tail[:, 2 * j:2 * j + 1], tail[:, 2 * j + 1:2 * j + 2])).astype(BF16)
        upd = _dot_tn(xt, bm[g])
        el = jnp.where(rowh, elast[:, 2 * j:2 * j + 1], elast[:, 2 * j + 1:2 * j + 2])
        ssm[128 * j:128 * j + 128, :] = el * hp + upd

    yz = y_s[0:L, :] * _silu(pj[r0:r0 + L, Z0:Z0 + 1024])
    for g in range(2):
        sl = slice(512 * g, 512 * g + 512)
        part = yz[:, sl]
        ms = jnp.mean(part * part, axis=-1, keepdims=True)
        mix_s[m0:m0 + L, sl] = (part * lax.rsqrt(ms + EPS) * snw[:, sl]).astype(mix_s.dtype)

    n_sub = L // LH
    AH = min(L, 2 * LH)
    n_att = L // AH
    r0v = hlb[0:1, :]
    r1v = hlb[1:2, :]
    mx = jnp.maximum(r0v, r1v)
    e0 = jnp.exp(r0v - mx)
    lb = e0 / (e0 + jnp.exp(r1v - mx))
    rid_w = lax.broadcasted_iota(I32, (L, 1024), 0)
    f = lb + (1.0 - lb) * jax.nn.sigmoid(pj[r0:r0 + L, F0:F0 + 1024])
    if nv < L:
        f = jnp.where(rid_w < nv, f, 1.0)
    kk = 1.0 - f
    blk = jnp.logical_and(causal, (rowi // LH) == (coli // LH)).astype(F32)
    b = _sel_dot(blk.astype(BF16), jnp.log(f))
    q = _silu(pj[r0:r0 + L, Q0:Q0 + 1024])
    base = jnp.zeros((1, 1024), F32)
    for i in range(n_sub):
        rows = slice(LH * i, LH * i + LH)
        bi = b[rows, :] + base
        b_s[rows, :] = bi
        qb_s[rows, :] = (q[rows, :] * jnp.exp(bi)).astype(BF16)
        base = bi[LH - 1:LH, :]
    for i in range(n_att):
        rows = slice(AH * i, AH * i + AH)
        mid = b_s[AH * i + AH // 2 - 1:AH * i + AH // 2, :]
        r_s[i:i + 1, :] = mid
        q_s[rows, :] = (q[rows, :] * jnp.exp(b_s[rows, :] - mid)).astype(BF16)
    kk_s[0:L, :] = kk
    v_s[0:L, :] = pj[r0:r0 + L, I0:I0 + 1024].astype(BF16)
    kend_s[0:L, :] = (kk * jnp.exp(base - b_s[0:L, :])).astype(BF16)
    ebl = jnp.exp(base)
    colq = lax.broadcasted_iota(I32, (AH, L), 1)
    rowq = lax.broadcasted_iota(I32, (AH, L), 0)

    for hh in range(HGRN_HEADS):
        c = slice(128 * hh, 128 * hh + 128)
        bh = b_s[0:L, c]
        kkh = kk_s[0:L, c]
        parts = []
        for i in range(n_att):
            n_keys = AH * i + AH
            ktil = (kkh[0:n_keys, :] * jnp.exp(r_s[i:i + 1, c] - bh[0:n_keys, :])).astype(BF16)
            if n_keys < L:
                ktil = jnp.concatenate([ktil, jnp.zeros((L - n_keys, LANE), BF16)], axis=0)
            att = _dot_nt(q_s[AH * i:AH * i + AH, c], ktil)
            parts.append(jnp.where(colq <= rowq + AH * i, att, 0.0).astype(BF16))
        att_all = parts[0] if n_att == 1 else jnp.concatenate(parts, axis=0)
        st = hst[c, :]
        oh = _dot(att_all, v_s[0:L, c]) + _dot_nt(qb_s[0:L, c], st.astype(BF16))
        hst[c, :] = ebl[:, c] * st + _dot_tn(v_s[0:L, c], kend_s[0:L, c])
        ms = jnp.mean(oh * oh, axis=-1, keepdims=True)
        on = oh * lax.rsqrt(ms + EPS) * hnw[:, c]
        mix_s[m0:m0 + L, 1024 + 128 * hh:1152 + 128 * hh] = (
            on * _silu(pj[r0:r0 + L, G0 + 128 * hh:G0 + 128 * hh + 128])).astype(mix_s.dtype)


def _chunk_scratch(rows):
    wide = lambda dt: pltpu.VMEM((rows, 1024), dt)
    return [wide(F32), wide(F32), wide(F32), pltpu.VMEM((MAX_SUB, 1024), F32), wide(BF16), wide(BF16), wide(BF16),
            wide(BF16)]


ROW_TILES = D_MODEL // LANE


def _store_rows(ref, val):
    for s in range(ROW_TILES):
        ref[:, s, :] = val[:, LANE * s:LANE * s + LANE]


def _load_rows(ref):
    planes = jnp.swapaxes(ref[...], 0, 1)
    return [planes[s] for s in range(ROW_TILES)]


def _out_part(mix, x, wout_ref, n2_ref, wrt_ref, br_ref):
    x1 = x + _dot(mix, wout_ref[...])
    ms = jnp.mean(x1 * x1, axis=-1, keepdims=True)
    xn2 = x1 * lax.rsqrt(ms + EPS) * n2_ref[...]
    lgt = _dot_nt(wrt_ref[...], xn2, precision=HI) + br_ref[...]
    return x1, xn2, lgt


def _prompt_kernel(tt, nt, x_ref, n1_ref, win_ref, cw, cb, dtb, alog, dskip, snw, hlb, hnw, wout_ref, n2_ref, wrt_ref,
                   br_ref, x1_ref, xn2_ref, lg_ref, conv_ref, ssm_ref, hgrn_ref,
                   pj, mix_s, hst, *scr):
    t = pl.program_id(1)

    @pl.when(t == 0)
    def _():
        pj[0:HALO, :] = jnp.zeros((HALO, PW), F32)
        ssm_ref[0] = jnp.zeros((SSD_HEADS * SSD_P, SSD_N), F32)
        hst[...] = jnp.zeros_like(hst)

    x = x_ref[0]
    _in_proj(x, n1_ref, win_ref, pj, HALO)
    prm = (cw, cb, dtb, alog, dskip, snw, hlb, hnw)
    n_scr = len(scr) // (tt // SSD_CHUNK)
    for c in range(tt // SSD_CHUNK):
        _mix_chunk(pj, HALO + SSD_CHUNK * c, SSD_CHUNK, SSD_CHUNK, HGRN_CHUNK, ssm_ref.at[0], hst, prm,
                   scr[n_scr * c:n_scr * c + n_scr], mix_s, SSD_CHUNK * c)
    x1, xn2, lgt = _out_part(mix_s[...], x, wout_ref, n2_ref, wrt_ref, br_ref)
    x1_ref[...] = x1
    _store_rows(xn2_ref, xn2)
    lg_ref[...] = lgt

    @pl.when(t == nt - 1)
    def _():
        conv_ref[0] = pj[HALO + tt - 3:HALO + tt, X0:X0 + CONV_DIM]
        for hh in range(HGRN_HEADS):
            c = slice(128 * hh, 128 * hh + 128)
            hgrn_ref[0, c, :] = hst[c, :].T

    pj[0:HALO, X0:X0 + CONV_DIM] = pj[tt:tt + HALO, X0:X0 + CONV_DIM]


def _const_spec(shape):
    nd = len(shape)
    return pl.BlockSpec(shape, lambda *_: (0,) * nd, pipeline_mode=pl.Buffered(1))


def _prompt_mixer(x, consts, tt=256):
    bsz, seq, _ = x.shape
    nt = seq // tt
    ta = bsz * seq
    const_specs = [_const_spec(c.shape) for c in consts]
    out_shape = (
        jax.ShapeDtypeStruct((ta, D_MODEL), F32),
        jax.ShapeDtypeStruct((ta, ROW_TILES, LANE), F32),
        jax.ShapeDtypeStruct((N_EXPERTS, ta), F32),
        jax.ShapeDtypeStruct((bsz, CONV_W - 1, CONV_DIM), F32),
        jax.ShapeDtypeStruct((bsz, SSD_HEADS * SSD_P, SSD_N), F32),
        jax.ShapeDtypeStruct((bsz, HGRN_HEADS * HGRN_D, HGRN_D), F32),
    )
    out_specs = (
        pl.BlockSpec((tt, D_MODEL), lambda b, t: (b * nt + t, 0)),
        pl.BlockSpec((tt, ROW_TILES, LANE), lambda b, t: (b * nt + t, 0, 0)),
        pl.BlockSpec((N_EXPERTS, tt), lambda b, t: (0, b * nt + t)),
        pl.BlockSpec((1, CONV_W - 1, CONV_DIM), lambda b, t: (b, 0, 0)),
        pl.BlockSpec((1, SSD_HEADS * SSD_P, SSD_N), lambda b, t: (b, 0, 0)),
        pl.BlockSpec((1, HGRN_HEADS * HGRN_D, HGRN_D), lambda b, t: (b, 0, 0)),
    )
    scratch = [
        pltpu.VMEM((HALO + tt, PW), F32),
        pltpu.VMEM((tt, 2048), BF16),
        pltpu.VMEM((HGRN_HEADS * HGRN_D, HGRN_D), F32),
    ] + _chunk_scratch(SSD_CHUNK) * (tt // SSD_CHUNK)
    return pl.pallas_call(
        functools.partial(_prompt_kernel, tt, nt),
        grid=(bsz, nt),
        in_specs=[pl.BlockSpec((1, tt, D_MODEL), lambda b, t: (b, t, 0))] + const_specs,
        out_specs=out_specs,
        out_shape=out_shape,
        scratch_shapes=scratch,
        compiler_params=pltpu.CompilerParams(dimension_semantics=("arbitrary", "arbitrary"),
                                             vmem_limit_bytes=VMEM_LIMIT),
        name="prompt_mixer",
    )(x, *consts)


def _sample_inproj_kernel(x_ref, n1_ref, win_ref, o_ref):
    _in_proj(x_ref[...], n1_ref, win_ref, o_ref, 0)


def _sample_inproj(x, n1, win, rows=256):
    n = x.shape[0]
    return pl.pallas_call(
        _sample_inproj_kernel,
        grid=(n // rows,),
        in_specs=[pl.BlockSpec((rows, D_MODEL), lambda i: (i, 0)), _const_spec(n1.shape), _const_spec(win.shape)],
        out_specs=pl.BlockSpec((rows, PW), lambda i: (i, 0)),
        out_shape=jax.ShapeDtypeStruct((n, PW), F32),
        compiler_params=pltpu.CompilerParams(dimension_semantics=("arbitrary",), vmem_limit_bytes=VMEM_LIMIT),
        name="sample_inproj",
    )(x, n1, win)


SAMPLE_L = 16


SAMPLE_GROUP = 4


def _sample_step_kernel(tdec, pj_ref, cin_ref, sin_ref, hin_ref, cw, cb, dtb, alog, dskip, snw, hlb, hnw,
                        mix_ref, cout_ref, sout_ref, hout_ref, *scr):
    prm = (cw, cb, dtb, alog, dskip, snw, hlb, hnw)
    n_scr = len(scr) // SAMPLE_GROUP
    for g in range(SAMPLE_GROUP):
        pj, mix_s, hst, *chunk_scr = scr[n_scr * g:n_scr * g + n_scr]
        pj[...] = jnp.zeros_like(pj)
        pj[HALO - 3:HALO, X0:X0 + CONV_DIM] = cin_ref[g]
        pj[HALO:HALO + tdec, :] = pj_ref[g]
        sout_ref[g] = sin_ref[g]
        for hh in range(HGRN_HEADS):
            c = slice(128 * hh, 128 * hh + 128)
            hst[c, :] = hin_ref[g, c, :].T
        _mix_chunk(pj, HALO, SAMPLE_L, tdec, SAMPLE_L, sout_ref.at[g], hst, prm, chunk_scr, mix_s, 0)
        mix_ref[g] = mix_s[0:tdec, :]
        cout_ref[g] = pj[HALO + tdec - 3:HALO + tdec, X0:X0 + CONV_DIM]
        for hh in range(HGRN_HEADS):
            c = slice(128 * hh, 128 * hh + 128)
            hout_ref[g, c, :] = hst[c, :].T


def _sample_step(proj, conv_in, ssm_in, hgrn_in, consts):
    nseq, tdec, _ = proj.shape
    assert nseq % SAMPLE_GROUP == 0
    seq_spec = lambda shape: pl.BlockSpec((SAMPLE_GROUP,) + shape, lambda i: (i,) + (0,) * len(shape))
    scratch = ([
        pltpu.VMEM((HALO + SAMPLE_L, PW), F32),
        pltpu.VMEM((SAMPLE_L, 2048), F32),
        pltpu.VMEM((HGRN_HEADS * HGRN_D, HGRN_D), F32),
    ] + _chunk_scratch(SAMPLE_L)) * SAMPLE_GROUP
    return pl.pallas_call(
        functools.partial(_sample_step_kernel, tdec),
        grid=(nseq // SAMPLE_GROUP,),
        in_specs=[seq_spec((tdec, PW)), seq_spec((CONV_W - 1, CONV_DIM)), seq_spec((SSD_HEADS * SSD_P, SSD_N)),
                  seq_spec((HGRN_HEADS * HGRN_D, HGRN_D))] + [_const_spec(c.shape) for c in consts],
        out_specs=(seq_spec((tdec, 2048)), seq_spec((CONV_W - 1, CONV_DIM)), seq_spec((SSD_HEADS * SSD_P, SSD_N)),
                   seq_spec((HGRN_HEADS * HGRN_D, HGRN_D))),
        out_shape=(jax.ShapeDtypeStruct((nseq, tdec, 2048), F32),
                   jax.ShapeDtypeStruct((nseq, CONV_W - 1, CONV_DIM), F32),
                   jax.ShapeDtypeStruct((nseq, SSD_HEADS * SSD_P, SSD_N), F32),
                   jax.ShapeDtypeStruct((nseq, HGRN_HEADS * HGRN_D, HGRN_D), F32)),
        scratch_shapes=scratch,
        compiler_params=pltpu.CompilerParams(dimension_semantics=("arbitrary",), vmem_limit_bytes=VMEM_LIMIT),
        name="sample_step",
    )(proj, conv_in, ssm_in, hgrn_in, *consts)


def _sample_out_kernel(mix_ref, x_ref, wout_ref, n2_ref, wrt_ref, br_ref, x1_ref, xn2_ref, lg_ref):
    x1, xn2, lgt = _out_part(mix_ref[...].astype(BF16), x_ref[...], wout_ref, n2_ref, wrt_ref, br_ref)
    x1_ref[...] = x1
    _store_rows(xn2_ref, xn2)
    lg_ref[...] = lgt


def _sample_out(mix, x, wout, n2, wrt, br):
    n = x.shape[0]
    return pl.pallas_call(
        _sample_out_kernel,
        out_shape=(jax.ShapeDtypeStruct((n, D_MODEL), F32), jax.ShapeDtypeStruct((n, ROW_TILES, LANE), F32),
                   jax.ShapeDtypeStruct((N_EXPERTS, n), F32)),
        compiler_params=pltpu.CompilerParams(vmem_limit_bytes=VMEM_LIMIT),
        name="sample_out",
    )(mix, x, wout, n2, wrt, br)


ROUTE_TILE = 256


def _route_kernel(n_first, lga_ref, lgb_ref, g_ref, d_ref, cnt_ref, cnt, carry, pstart):
    ph = pl.program_id(0)
    i = pl.program_id(1)
    tl = ROUTE_TILE
    l = jnp.where(i < n_first, lga_ref[...], lgb_ref[...])
    eid = lax.broadcasted_iota(I32, (N_EXPERTS, tl), 0)
    hots, vals = [], []
    for _ in range(TOP_K):
        m = jnp.max(l, axis=0, keepdims=True)
        idx = jnp.min(jnp.where(l == m, eid, N_EXPERTS), axis=0, keepdims=True)
        hot = eid == idx
        hots.append(hot)
        vals.append(m)
        l = jnp.where(hot, -jnp.inf, l)
    ind = hots[0].astype(F32)
    for hot in hots[1:]:
        ind = ind + hot.astype(F32)
    tile_cnt = jnp.sum(ind, axis=1, keepdims=True)

    @pl.when(jnp.logical_and(ph == 0, i == 0))
    def _():
        cnt[...] = jnp.zeros_like(cnt)

    @pl.when(ph == 0)
    def _():
        cnt[...] = cnt[...] + tile_cnt

    @pl.when(jnp.logical_and(ph == 1, i == 0))
    def _():
        padded = jnp.floor((cnt[...] + (MOE_BLOCK - 1)) * (1.0 / MOE_BLOCK)) * MOE_BLOCK
        r = lax.broadcasted_iota(I32, (N_EXPERTS, N_EXPERTS), 0)
        c = lax.broadcasted_iota(I32, (N_EXPERTS, N_EXPERTS), 1)
        pend = jnp.dot((c <= r).astype(F32), padded, precision=HI, preferred_element_type=F32)
        pstart[...] = pend - padded
        carry[...] = jnp.zeros_like(carry)
        cnt_ref[...] = cnt[...]

    @pl.when(ph == 1)
    def _():
        rr = lax.broadcasted_iota(I32, (tl, tl), 0)
        cc = lax.broadcasted_iota(I32, (tl, tl), 1)
        before = _dot(ind.astype(BF16), (rr < cc).astype(BF16))
        base = before + carry[:, 0:1] + pstart[:, 0:1]
        den = jnp.exp(vals[0] - vals[0])
        for k in range(1, TOP_K):
            den = den + jnp.exp(vals[k] - vals[0])
        g_ref[...] = jnp.zeros_like(g_ref)
        for k in range(TOP_K):
            g_ref[k:k + 1, :] = jnp.exp(vals[k] - vals[0]) / den
            d_ref[k:k + 1, :] = jnp.sum(jnp.where(hots[k], base, 0.0), axis=0, keepdims=True).astype(I32)
        carry[...] = carry[...] + tile_cnt


def _two_source_maps(n_first, trailing=1):
    pad = (0,) * trailing
    return (lambda *idx: (jnp.minimum(idx[-1], n_first - 1),) + pad,
            lambda *idx: (jnp.maximum(idx[-1] - n_first, 0),) + pad)


def _route(lg_a, lg_b):
    ta = lg_a.shape[1] + lg_b.shape[1]
    nt = ta // ROUTE_TILE
    n_first = lg_a.shape[1] // ROUTE_TILE
    assert n_first * ROUTE_TILE == lg_a.shape[1] and nt * ROUTE_TILE == ta
    return pl.pallas_call(
        functools.partial(_route_kernel, n_first),
        grid=(2, nt),
        in_specs=[pl.BlockSpec((N_EXPERTS, ROUTE_TILE), lambda p, i: (0, jnp.minimum(i, n_first - 1))),
                  pl.BlockSpec((N_EXPERTS, ROUTE_TILE), lambda p, i: (0, jnp.maximum(i - n_first, 0)))],
        out_specs=(pl.BlockSpec((8, ROUTE_TILE), lambda p, i: (0, i * p)),
                   pl.BlockSpec((None, TOP_K, ROUTE_TILE), lambda p, i: (i * p, 0, 0)),
                   pl.BlockSpec((N_EXPERTS, LANE), lambda p, i: (0, 0))),
        out_shape=(jax.ShapeDtypeStruct((8, ta), F32), jax.ShapeDtypeStruct((nt, TOP_K, ROUTE_TILE), I32),
                   jax.ShapeDtypeStruct((N_EXPERTS, LANE), F32)),
        scratch_shapes=[pltpu.VMEM((N_EXPERTS, LANE), F32)] * 3,
        compiler_params=pltpu.CompilerParams(dimension_semantics=("arbitrary", "arbitrary")),
        name="route",
    )(lg_a, lg_b)


DMA_UNROLL = 8


def _row_copy(src, dst, sem):
    return pltpu.make_async_copy(src, dst, sem)


def _dispatch_kernel(n_first, pend_ref, padded_ref, nu_ref, d_ref, xa_ref, xb_ref, o_ref, zeros, sem):
    i = pl.program_id(0)
    tl = ROUTE_TILE
    nb = o_ref.shape[0] // MOE_BLOCK

    @pl.when(i == 0)
    def _():
        zeros[...] = jnp.zeros_like(zeros)

        def fills(e):
            return ((padded_ref[e] > 0, pend_ref[e] - MOE_BLOCK),
                    (nu_ref[0] + e < nb, (nu_ref[0] + e) * MOE_BLOCK))

        for e in range(N_EXPERTS):
            for cond, row in fills(e):
                @pl.when(cond)
                def _(row=row):
                    _row_copy(zeros, o_ref.at[pl.ds(row, MOE_BLOCK)], sem).start()
        for e in range(N_EXPERTS):
            for cond, row in fills(e):
                @pl.when(cond)
                def _(row=row):
                    _row_copy(zeros, o_ref.at[pl.ds(row, MOE_BLOCK)], sem).wait()

    def scatter_rows(x_ref):
        def start(j2, c):
            for u in range(DMA_UNROLL):
                j = j2 * DMA_UNROLL + u
                for k in range(TOP_K):
                    _row_copy(x_ref.at[j], o_ref.at[d_ref[k, j]], sem).start(priority=(u + k) % 2)
            return c

        lax.fori_loop(0, tl // DMA_UNROLL, start, 0)

        def wait(j2, c):
            for u in range(DMA_UNROLL):
                j = j2 * DMA_UNROLL + u
                for k in range(TOP_K):
                    _row_copy(x_ref.at[j], o_ref.at[d_ref[k, j]], sem).wait()
            return c

        lax.fori_loop(0, tl // DMA_UNROLL, wait, 0)

    @pl.when(i < n_first)
    def _():
        scatter_rows(xa_ref)

    @pl.when(i >= n_first)
    def _():
        scatter_rows(xb_ref)


def _dispatch(xa, xb, d3, pend, padded, n_used, n_rows):
    nt = (xa.shape[0] + xb.shape[0]) // ROUTE_TILE
    n_first = xa.shape[0] // ROUTE_TILE
    map_a, map_b = _two_source_maps(n_first, trailing=2)
    grid_spec = pltpu.PrefetchScalarGridSpec(
        num_scalar_prefetch=3,
        grid=(nt,),
        in_specs=[pl.BlockSpec((None, TOP_K, ROUTE_TILE), lambda i, *_: (i, 0, 0), memory_space=pltpu.SMEM),
                  pl.BlockSpec((ROUTE_TILE, ROW_TILES, LANE), lambda i, *_: map_a(i)),
                  pl.BlockSpec((ROUTE_TILE, ROW_TILES, LANE), lambda i, *_: map_b(i))],
        out_specs=pl.BlockSpec(memory_space=pl.ANY),
        scratch_shapes=[pltpu.VMEM((MOE_BLOCK, ROW_TILES, LANE), F32), pltpu.SemaphoreType.DMA],
    )
    return pl.pallas_call(
        functools.partial(_dispatch_kernel, n_first),
        grid_spec=grid_spec,
        out_shape=jax.ShapeDtypeStruct((n_rows, ROW_TILES, LANE), F32),
        compiler_params=pltpu.CompilerParams(dimension_semantics=("arbitrary",)),
        name="dispatch",
    )(pend, padded, n_used, d3, xa, xb)


def _expert_kernel(be_ref, nu_ref, x_ref, wg_ref, bg_ref, wu_ref, bu_ref, wd_ref, bd_ref, o_ref, wg_s, wu_s, wd_s,
                   lhs_s):
    i = pl.program_id(0)
    prev = be_ref[jnp.maximum(i - 1, 0)]
    fresh = jnp.logical_or(i == 0, be_ref[i] != prev)

    @pl.when(jnp.logical_and(i < nu_ref[0], fresh))
    def _():
        wg_s[...] = wg_ref[0].astype(BF16)
        wu_s[...] = wu_ref[0].astype(BF16)
        wd_s[...] = wd_ref[0].astype(BF16)

    @pl.when(i < nu_ref[0])
    def _():
        for s, slab in enumerate(_load_rows(x_ref)):
            lhs_s[:, LANE * s:LANE * s + LANE] = slab.astype(BF16)
        x = lhs_s[...]
        g = _dot(x, wg_s[...]) + bg_ref[0]
        u = _dot(x, wu_s[...]) + bu_ref[0]
        g = jnp.minimum(g, SWIGLU_LIMIT)
        u = jnp.clip(u, -SWIGLU_LIMIT, SWIGLU_LIMIT)
        act = g * jax.nn.sigmoid(SWIGLU_ALPHA * g) * (u + 1.0)
        _store_rows(o_ref, _dot(act.astype(BF16), wd_s[...]) + bd_ref[0])

    @pl.when(i >= nu_ref[0])
    def _():
        o_ref[...] = jnp.zeros_like(o_ref)


def _experts(xs, block_e, n_used, wg, bg, wu, bu, wd, bd):
    n_rows = xs.shape[0]
    nb = n_rows // MOE_BLOCK
    row_map = lambda i, be, nu: (jnp.maximum(jnp.minimum(i, nu[0] - 1), 0), 0, 0)
    w_map = lambda i, be, nu: (be[i], 0, 0)
    w_spec = pl.BlockSpec((1, D_MODEL, D_MODEL), w_map)
    b_spec = pl.BlockSpec((1, 1, D_MODEL), w_map)
    grid_spec = pltpu.PrefetchScalarGridSpec(
        num_scalar_prefetch=2,
        grid=(nb,),
        in_specs=[pl.BlockSpec((MOE_BLOCK, ROW_TILES, LANE), row_map), w_spec, b_spec, w_spec, b_spec, w_spec,
                  b_spec],
        out_specs=pl.BlockSpec((MOE_BLOCK, ROW_TILES, LANE), lambda i, be, nu: (i, 0, 0)),
        scratch_shapes=[pltpu.VMEM((D_MODEL, D_MODEL), BF16)] * 3 + [pltpu.VMEM((MOE_BLOCK, D_MODEL), BF16)],
    )
    return pl.pallas_call(
        _expert_kernel,
        grid_spec=grid_spec,
        out_shape=jax.ShapeDtypeStruct((n_rows, ROW_TILES, LANE), F32),
        compiler_params=pltpu.CompilerParams(dimension_semantics=("arbitrary",), vmem_limit_bytes=VMEM_LIMIT),
        name="experts",
    )(block_e, n_used, xs, wg, bg.reshape(N_EXPERTS, 1, D_MODEL), wu, bu.reshape(N_EXPERTS, 1, D_MODEL), wd,
      bd.reshape(N_EXPERTS, 1, D_MODEL))


def _combine_kernel(n_first, d_ref, x1a_ref, x1b_ref, g_ref, nf_ref, eo_ref, ya_ref, yb_ref, buf, y_s, sem):
    i = pl.program_id(0)
    tl = ROUTE_TILE

    def gather(start):
        def body(j2, c):
            for u in range(DMA_UNROLL):
                j = j2 * DMA_UNROLL + u
                for k in range(TOP_K):
                    cp = _row_copy(eo_ref.at[d_ref[k, j]], buf.at[k, j], sem)
                    if start:
                        cp.start(priority=(u + k) % 2)
                    else:
                        cp.wait()
            return c

        lax.fori_loop(0, tl // DMA_UNROLL, body, 0)

    gather(True)
    sel = (lax.broadcasted_iota(I32, (8, LANE), 0) == lax.broadcasted_iota(I32, (8, LANE), 1)).astype(F32)
    gt = _dot_tn(g_ref[...], sel, precision=HI)
    gather(False)
    for k in range(TOP_K):
        for s, slab in enumerate(_load_rows(buf.at[k])):
            cols = slice(LANE * s, LANE * s + LANE)
            term = gt[:, k:k + 1] * slab
            y_s[:, cols] = term if k == 0 else y_s[:, cols] + term
    y = jnp.where(i < n_first, x1a_ref[...], x1b_ref[...]) + y_s[...]
    y = y * lax.rsqrt(jnp.mean(y * y, axis=-1, keepdims=True) + EPS) * nf_ref[...]

    @pl.when(i < n_first)
    def _():
        ya_ref[...] = y

    @pl.when(i >= n_first)
    def _():
        yb_ref[...] = y


def _combine(x1a, x1b, gates8, d3, eo3, nf):
    nt = (x1a.shape[0] + x1b.shape[0]) // ROUTE_TILE
    n_first = x1a.shape[0] // ROUTE_TILE
    map_a, map_b = _two_source_maps(n_first)
    row_specs = (pl.BlockSpec((ROUTE_TILE, D_MODEL), map_a), pl.BlockSpec((ROUTE_TILE, D_MODEL), map_b))
    grid_spec = pltpu.PrefetchScalarGridSpec(
        num_scalar_prefetch=0,
        grid=(nt,),
        in_specs=[pl.BlockSpec((None, TOP_K, ROUTE_TILE), lambda i: (i, 0, 0), memory_space=pltpu.SMEM),
                  *row_specs,
                  pl.BlockSpec((8, ROUTE_TILE), lambda i: (0, i)),
                  _const_spec(nf.shape),
                  pl.BlockSpec(memory_space=pl.ANY)],
        out_specs=row_specs,
        scratch_shapes=[pltpu.VMEM((TOP_K, ROUTE_TILE, ROW_TILES, LANE), F32),
                        pltpu.VMEM((ROUTE_TILE, D_MODEL), F32), pltpu.SemaphoreType.DMA],
    )
    return pl.pallas_call(
        functools.partial(_combine_kernel, n_first),
        grid_spec=grid_spec,
        out_shape=(jax.ShapeDtypeStruct(x1a.shape, F32), jax.ShapeDtypeStruct(x1b.shape, F32)),
        compiler_params=pltpu.CompilerParams(dimension_semantics=("arbitrary",), vmem_limit_bytes=VMEM_LIMIT),
        name="combine",
    )(d3, x1a, x1b, gates8, nf, eo3)


def _moe_and_norm(x1, xn2, lgt, wg, bg, wu, bu, wd, bd, nf):
    ta = x1[0].shape[0] + x1[1].shape[0]
    n_rows = (-(-(ta * TOP_K) // MOE_BLOCK) + N_EXPERTS) * MOE_BLOCK
    gates8, d3, cnt = _route(*lgt)
    counts = cnt[:, 0].astype(I32)
    padded = (counts + MOE_BLOCK - 1) // MOE_BLOCK * MOE_BLOCK
    pend = jnp.cumsum(padded)
    n_used = (pend[-1:] // MOE_BLOCK).astype(I32)
    block_row = jnp.arange(n_rows // MOE_BLOCK, dtype=I32) * MOE_BLOCK
    block_e = jnp.minimum(jnp.sum((pend[None, :] <= block_row[:, None]).astype(I32), axis=1), N_EXPERTS - 1)
    xs = _dispatch(*xn2, d3, pend.astype(I32), padded.astype(I32), n_used, n_rows)
    eo = _experts(xs, block_e, n_used, wg, bg, wu, bu, wd, bd)
    return _combine(*x1, gates8, d3, eo, nf)


def _prep_consts(norm1_w, w_in, conv_w, conv_b, dt_bias, A_log, D_skip, ssd_norm_w, hgrn_lower_bound, hgrn_norm_w,
                 w_out, norm2_w, w_router, b_router):
    row = lambda v: v.reshape(1, -1).astype(F32)
    pad_lane = lambda v: jnp.pad(v.reshape(1, -1).astype(F32), ((0, 0), (0, LANE - v.size)))
    wb = w_in.astype(BF16)
    win = jnp.concatenate([wb[:, :2560], wb[:, 2576:6672], wb[:, 2560:2576],
                           jnp.zeros((D_MODEL, LANE - SSD_HEADS), BF16)], axis=1)
    dskip = jnp.repeat(D_skip.astype(F32), SSD_P).reshape(1, -1)
    return (row(norm1_w), win, conv_w.astype(F32), row(conv_b), pad_lane(dt_bias), pad_lane(A_log), dskip,
            row(ssd_norm_w), hgrn_lower_bound.astype(F32), row(hgrn_norm_w), w_out.astype(BF16), row(norm2_w),
            w_router.T.astype(F32), b_router.reshape(-1, 1).astype(F32))


def kernel(x_prompt, x_sample, state_conv, state_ssm, state_hgrn, norm1_w, w_in, conv_w, conv_b, dt_bias, A_log, D_skip, ssd_norm_w, hgrn_lower_bound, hgrn_norm_w, w_out, norm2_w, w_router, b_router, w_gate, b_gate, w_up, b_up, w_down, b_down, norm_f_w):
    depth = w_in.shape[0]
    assert depth == 1
    bsz, seq, _ = x_prompt.shape
    nseq, tdec, _ = x_sample.shape
    consts = _prep_consts(norm1_w[0], w_in[0], conv_w[0], conv_b[0], dt_bias[0], A_log[0], D_skip[0], ssd_norm_w[0],
                          hgrn_lower_bound, hgrn_norm_w[0], w_out[0], norm2_w[0], w_router[0], b_router[0])
    n1, win, cw, cb, dtb, alog, dskip, snw, hlb, hnw, wout, n2, wrt, br = consts
    chunk_consts = (cw, cb, dtb, alog, dskip, snw, hlb, hnw)

    x1_p, xn2_p, lg_p, conv_p, ssm_p, hgrn_p = _prompt_mixer(x_prompt, consts)

    xs_flat = x_sample.reshape(nseq * tdec, D_MODEL)
    proj_s = _sample_inproj(xs_flat, n1, win).reshape(nseq, tdec, PW)
    mix_s, conv_s, ssm_s, hgrn_s = _sample_step(
        proj_s, state_conv[0], state_ssm[0].reshape(nseq, SSD_HEADS * SSD_P, SSD_N),
        state_hgrn[0].reshape(nseq, HGRN_HEADS * HGRN_D, HGRN_D), chunk_consts)
    x1_s, xn2_s, lg_s = _sample_out(mix_s.reshape(nseq * tdec, 2048), xs_flat, wout, n2, wrt, br)

    y_p, y_s = _moe_and_norm((x1_p, x1_s), (xn2_p, xn2_s), (lg_p, lg_s), w_gate[0], b_gate[0], w_up[0], b_up[0],
                             w_down[0], b_down[0], norm_f_w.reshape(1, -1).astype(F32))
    y_prompt = y_p.reshape(bsz, seq, D_MODEL)
    y_sample = y_s.reshape(nseq, tdec, D_MODEL)
    return (y_prompt, y_sample,
            conv_p[None], ssm_p.reshape(1, bsz, SSD_HEADS, SSD_P, SSD_N),
            hgrn_p.reshape(1, bsz, HGRN_HEADS, HGRN_D, HGRN_D),
            conv_s[None], ssm_s.reshape(1, nseq, SSD_HEADS, SSD_P, SSD_N),
            hgrn_s.reshape(1, nseq, HGRN_HEADS, HGRN_D, HGRN_D))
```

```python
import functools

import jax
import jax.numpy as jnp
from jax import lax
from jax.experimental import pallas as pl
from jax.experimental.pallas import tpu as pltpu

F32 = jnp.float32
BF16 = jnp.bfloat16
I32 = jnp.int32
HI = lax.Precision.HIGHEST

D_MODEL = 1024
SSD_HEADS = 16
SSD_P = 64
SSD_N = 128
CONV_DIM = 1536
CONV_W = 4
HGRN_HEADS = 8
HGRN_D = 128
N_EXPERTS = 32
TOP_K = 4
MOE_BLOCK = 512
SWIGLU_LIMIT = 7.0
SWIGLU_ALPHA = 1.702
EPS = 1e-6

Z0 = 0
X0 = 1024
Q0 = 2560
F0 = 3584
I0 = 4608
G0 = 5632
DT0 = 6656
PW = 6784
LANE = 128
HALO = 8
SSD_CHUNK = 128
HGRN_CHUNK = 16
MAX_SUB = SSD_CHUNK // HGRN_CHUNK
VMEM_LIMIT = 56 * 1024 * 1024


def _silu(x):
    return x * jax.nn.sigmoid(x)


def _softplus(x):
    return jnp.maximum(x, 0.0) + jnp.log1p(jnp.exp(-jnp.abs(x)))


def _dot(a, b):
    return jnp.dot(a, b, preferred_element_type=F32)


def _dot_nt(a, b, precision=None):
    return lax.dot_general(a, b, (((1,), (1,)), ((), ())), precision=precision, preferred_element_type=F32)


def _dot_tn(a, b, precision=None):
    return lax.dot_general(a, b, (((0,), (0,)), ((), ())), precision=precision, preferred_element_type=F32)


def _split3(x):
    hi = x.astype(BF16)
    r1 = x - hi.astype(F32)
    mid = r1.astype(BF16)
    lo = (r1 - mid.astype(F32)).astype(BF16)
    return hi, mid, lo


def _sel_dot(sel, x):
    return sum(_dot(sel, p) for p in _split3(x))


def _sel_dot_tn(x, sel):
    return sum(_dot_tn(p, sel) for p in _split3(x))


def _in_proj(x, n1_ref, win_ref, pj, r0):
    rows = x.shape[0]
    ms = jnp.mean(x * x, axis=-1, keepdims=True)
    h = (x * lax.rsqrt(ms + EPS) * n1_ref[...]).astype(BF16)
    for c0 in range(0, PW, 512):
        c1 = min(c0 + 512, PW)
        pj[r0:r0 + rows, c0:c1] = _dot(h, win_ref[:, c0:c1])


def _mix_chunk(pj, r0, L, nv, LH, ssm, hst, prm, scr, mix_s, m0):
    cw, cb, dtb, alog, dskip, snw, hlb, hnw = prm
    y_s, b_s, kk_s, r_s, q_s, qb_s, v_s, kend_s = scr

    def conv_cols(c0, c1):
        acc = cb[:, c0:c1]
        for k in range(CONV_W):
            acc = acc + pj[r0 - 3 + k:r0 - 3 + k + L, X0 + c0:X0 + c1] * cw[k:k + 1, c0:c1]
        return _silu(acc)

    rowi = lax.broadcasted_iota(I32, (L, L), 0)
    coli = lax.broadcasted_iota(I32, (L, L), 1)
    causal = coli <= rowi
    rid = lax.broadcasted_iota(I32, (L, LANE), 0)
    lane = lax.broadcasted_iota(I32, (L, LANE), 1)
    lo = lane < SSD_P

    dt = _softplus(pj[r0:r0 + L, DT0:DT0 + LANE] + dtb[...])
    if nv < L:
        dt = jnp.where(rid < nv, dt, 0.0)
    a = dt * (-jnp.exp(alog[...]))
    acum = _sel_dot(causal.astype(BF16), a)
    acum_t = _sel_dot_tn(a, (rowi <= coli).astype(BF16))
    dt_t = _sel_dot_tn(dt, (rowi == coli).astype(BF16))
    eac = jnp.exp(acum)
    last = acum[L - 1:L, :]
    elast = jnp.exp(last)
    tail = jnp.exp(last - acum) * dt

    bm = [conv_cols(1024 + 128 * g, 1152 + 128 * g).astype(BF16) for g in range(2)]
    cm = [conv_cols(1280 + 128 * g, 1408 + 128 * g).astype(BF16) for g in range(2)]
    gcb = [_dot_nt(cm[g], bm[g]) for g in range(2)]
    rowh = lax.broadcasted_iota(I32, (LANE, LANE), 0) < SSD_P

    for j in range(SSD_HEADS // 2):
        g = j // 4
        cols = slice(128 * j, 128 * j + 128)
        xs = conv_cols(128 * j, 128 * j + 128)
        xsb = xs.astype(BF16)
        ys = []
        for h in (2 * j, 2 * j + 1):
            seg = acum[:, h:h + 1] - acum_t[h:h + 1, :]
            dec = jnp.exp(jnp.where(causal, seg, -jnp.inf))
            mh = (gcb[g] * dec * dt_t[h:h + 1, :]).astype(BF16)
            ys.append(_dot(mh, xsb))
        y = jnp.where(lo, ys[0], ys[1])
        hp = ssm[128 * j:128 * j + 128, :]
        yoff = _dot_nt(cm[g], hp.astype(BF16))
        y = y + yoff * jnp.where(lo, eac[:, 2 * j:2 * j + 1], eac[:, 2 * j + 1:2 * j + 2])
        y = y + dskip[:, cols] * xs
        y_s[0:L, cols] = y
        xt = (xs * jnp.where(lo, tail[:, 2 * j:2 * j + 1], tail[:, 2 * j + 1:2 * j + 2])).astype(BF16)
        upd = _dot_tn(xt, bm[g])
        el = jnp.where(rowh, elast[:, 2 * j:2 * j + 1], elast[:, 2 * j + 1:2 * j + 2])
        ssm[128 * j:128 * j + 128, :] = el * hp + upd

    yz = y_s[0:L, :] * _silu(pj[r0:r0 + L, Z0:Z0 + 1024])
    for g in range(2):
        sl = slice(512 * g, 512 * g + 512)
        part = yz[:, sl]
        ms = jnp.mean(part * part, axis=-1, keepdims=True)
        mix_s[m0:m0 + L, sl] = (part * lax.rsqrt(ms + EPS) * snw[:, sl]).astype(mix_s.dtype)

    n_sub = L // LH
    AH = min(L, 2 * LH)
    n_att = L // AH
    r0v = hlb[0:1, :]
    r1v = hlb[1:2, :]
    mx = jnp.maximum(r0v, r1v)
    e0 = jnp.exp(r0v - mx)
    lb = e0 / (e0 + jnp.exp(r1v - mx))
    rid_w = lax.broadcasted_iota(I32, (L, 1024), 0)
    f = lb + (1.0 - lb) * jax.nn.sigmoid(pj[r0:r0 + L, F0:F0 + 1024])
    if nv < L:
        f = jnp.where(rid_w < nv, f, 1.0)
    kk = 1.0 - f
    blk = jnp.logical_and(causal, (rowi // LH) == (coli // LH)).astype(F32)
    b = _sel_dot(blk.astype(BF16), jnp.log(f))
    q = _silu(pj[r0:r0 + L, Q0:Q0 + 1024])
    base = jnp.zeros((1, 1024), F32)
    for i in range(n_sub):
        rows = slice(LH * i, LH * i + LH)
        bi = b[rows, :] + base
        b_s[rows, :] = bi
        qb_s[rows, :] = (q[rows, :] * jnp.exp(bi)).astype(BF16)
        base = bi[LH - 1:LH, :]
    for i in range(n_att):
        rows = slice(AH * i, AH * i + AH)
        mid = b_s[AH * i + AH // 2 - 1:AH * i + AH // 2, :]
        r_s[i:i + 1, :] = mid
        q_s[rows, :] = (q[rows, :] * jnp.exp(b_s[rows, :] - mid)).astype(BF16)
    kk_s[0:L, :] = kk
    v_s[0:L, :] = pj[r0:r0 + L, I0:I0 + 1024].astype(BF16)
    kend_s[0:L, :] = (kk * jnp.exp(base - b_s[0:L, :])).astype(BF16)
    ebl = jnp.exp(base)
    colq = lax.broadcasted_iota(I32, (AH, L), 1)
    rowq = lax.broadcasted_iota(I32, (AH, L), 0)

    for hh in range(HGRN_HEADS):
        c = slice(128 * hh, 128 * hh + 128)
        bh = b_s[0:L, c]
        kkh = kk_s[0:L, c]
        parts = []
        for i in range(n_att):
            n_keys = AH * i + AH
            ktil = (kkh[0:n_keys, :] * jnp.exp(r_s[i:i + 1, c] - bh[0:n_keys, :])).astype(BF16)
            if n_keys < L:
                ktil = jnp.concatenate([ktil, jnp.zeros((L - n_keys, LANE), BF16)], axis=0)
            att = _dot_nt(q_s[AH * i:AH * i + AH, c], ktil)
            parts.append(jnp.where(colq <= rowq + AH * i, att, 0.0).astype(BF16))
        att_all = parts[0] if n_att == 1 else jnp.concatenate(parts, axis=0)
        st = hst[c, :]
        oh = _dot(att_all, v_s[0:L, c]) + _dot_nt(qb_s[0:L, c], st.astype(BF16))
        hst[c, :] = ebl[:, c] * st + _dot_tn(v_s[0:L, c], kend_s[0:L, c])
        ms = jnp.mean(oh * oh, axis=-1, keepdims=True)
        on = oh * lax.rsqrt(ms + EPS) * hnw[:, c]
        mix_s[m0:m0 + L, 1024 + 128 * hh:1152 + 128 * hh] = (
            on * _silu(pj[r0:r0 + L, G0 + 128 * hh:G0 + 128 * hh + 128])).astype(mix_s.dtype)


def _chunk_scratch(rows):
    wide = lambda dt: pltpu.VMEM((rows, 1024), dt)
    return [wide(F32), wide(F32), wide(F32), pltpu.VMEM((MAX_SUB, 1024), F32), wide(BF16), wide(BF16), wide(BF16),
            wide(BF16)]


ROW_TILES = D_MODEL // LANE


def _store_rows(ref, val):
    for s in range(ROW_TILES):
        ref[:, s, :] = val[:, LANE * s:LANE * s + LANE]


def _load_rows(ref):
    planes = jnp.swapaxes(ref[...], 0, 1)
    return [planes[s] for s in range(ROW_TILES)]


def _out_part(mix, x, wout_ref, n2_ref, wrt_ref, br_ref):
    x1 = x + _dot(mix, wout_ref[...])
    ms = jnp.mean(x1 * x1, axis=-1, keepdims=True)
    xn2 = x1 * lax.rsqrt(ms + EPS) * n2_ref[...]
    lgt = _dot_nt(wrt_ref[...], xn2, precision=HI) + br_ref[...]
    return x1, xn2, lgt


def _prompt_kernel(tt, nt, x_ref, n1_ref, win_ref, cw, cb, dtb, alog, dskip, snw, hlb, hnw, wout_ref, n2_ref, wrt_ref,
                   br_ref, x1_ref, xn2_ref, lg_ref, conv_ref, ssm_ref, hgrn_ref,
                   pj, mix_s, hst, *scr):
    t = pl.program_id(1)

    @pl.when(t == 0)
    def _():
        pj[0:HALO, :] = jnp.zeros((HALO, PW), F32)
        ssm_ref[0] = jnp.zeros((SSD_HEADS * SSD_P, SSD_N), F32)
        hst[...] = jnp.zeros_like(hst)

    x = x_ref[0]
    _in_proj(x, n1_ref, win_ref, pj, HALO)
    prm = (cw, cb, dtb, alog, dskip, snw, hlb, hnw)
    n_scr = len(scr) // (tt // SSD_CHUNK)
    for c in range(tt // SSD_CHUNK):
        _mix_chunk(pj, HALO + SSD_CHUNK * c, SSD_CHUNK, SSD_CHUNK, HGRN_CHUNK, ssm_ref.at[0], hst, prm,
                   scr[n_scr * c:n_scr * c + n_scr], mix_s, SSD_CHUNK * c)
    x1, xn2, lgt = _out_part(mix_s[...], x, wout_ref, n2_ref, wrt_ref, br_ref)
    x1_ref[...] = x1
    _store_rows(xn2_ref, xn2)
    lg_ref[...] = lgt

    @pl.when(t == nt - 1)
    def _():
        conv_ref[0] = pj[HALO + tt - 3:HALO + tt, X0:X0 + CONV_DIM]
        for hh in range(HGRN_HEADS):
            c = slice(128 * hh, 128 * hh + 128)
            hgrn_ref[0, c, :] = hst[c, :].T

    pj[0:HALO, X0:X0 + CONV_DIM] = pj[tt:tt + HALO, X0:X0 + CONV_DIM]


def _const_spec(shape):
    nd = len(shape)
    return pl.BlockSpec(shape, lambda *_: (0,) * nd, pipeline_mode=pl.Buffered(1))


def _prompt_mixer(x, consts, tt=256):
    bsz, seq, _ = x.shape
    nt = seq // tt
    ta = bsz * seq
    const_specs = [_const_spec(c.shape) for c in consts]
    out_shape = (
        jax.ShapeDtypeStruct((ta, D_MODEL), F32),
        jax.ShapeDtypeStruct((ta, ROW_TILES, LANE), F32),
        jax.ShapeDtypeStruct((N_EXPERTS, ta), F32),
        jax.ShapeDtypeStruct((bsz, CONV_W - 1, CONV_DIM), F32),
        jax.ShapeDtypeStruct((bsz, SSD_HEADS * SSD_P, SSD_N), F32),
        jax.ShapeDtypeStruct((bsz, HGRN_HEADS * HGRN_D, HGRN_D), F32),
    )
    out_specs = (
        pl.BlockSpec((tt, D_MODEL), lambda b, t: (b * nt + t, 0)),
        pl.BlockSpec((tt, ROW_TILES, LANE), lambda b, t: (b * nt + t, 0, 0)),
        pl.BlockSpec((N_EXPERTS, tt), lambda b, t: (0, b * nt + t)),
        pl.BlockSpec((1, CONV_W - 1, CONV_DIM), lambda b, t: (b, 0, 0)),
        pl.BlockSpec((1, SSD_HEADS * SSD_P, SSD_N), lambda b, t: (b, 0, 0)),
        pl.BlockSpec((1, HGRN_HEADS * HGRN_D, HGRN_D), lambda b, t: (b, 0, 0)),
    )
    scratch = [
        pltpu.VMEM((HALO + tt, PW), F32),
        pltpu.VMEM((tt, 2048), BF16),
        pltpu.VMEM((HGRN_HEADS * HGRN_D, HGRN_D), F32),
    ] + _chunk_scratch(SSD_CHUNK) * (tt // SSD_CHUNK)
    return pl.pallas_call(
        functools.partial(_prompt_kernel, tt, nt),
        grid=(bsz, nt),
        in_specs=[pl.BlockSpec((1, tt, D_MODEL), lambda b, t: (b, t, 0))] + const_specs,
        out_specs=out_specs,
        out_shape=out_shape,
        scratch_shapes=scratch,
        compiler_params=pltpu.CompilerParams(dimension_semantics=("arbitrary", "arbitrary"),
                                             vmem_limit_bytes=VMEM_LIMIT),
        name="prompt_mixer",
    )(x, *consts)


def _sample_inproj_kernel(x_ref, n1_ref, win_ref, o_ref):
    _in_proj(x_ref[...], n1_ref, win_ref, o_ref, 0)


def _sample_inproj(x, n1, win, rows=256):
    n = x.shape[0]
    return pl.pallas_call(
        _sample_inproj_kernel,
        grid=(n // rows,),
        in_specs=[pl.BlockSpec((rows, D_MODEL), lambda i: (i, 0)), _const_spec(n1.shape), _const_spec(win.shape)],
        out_specs=pl.BlockSpec((rows, PW), lambda i: (i, 0)),
        out_shape=jax.ShapeDtypeStruct((n, PW), F32),
        compiler_params=pltpu.CompilerParams(dimension_semantics=("arbitrary",), vmem_limit_bytes=VMEM_LIMIT),
        name="sample_inproj",
    )(x, n1, win)


SAMPLE_L = 16


SAMPLE_GROUP = 4


def _sample_step_kernel(tdec, pj_ref, cin_ref, sin_ref, hin_ref, cw, cb, dtb, alog, dskip, snw, hlb, hnw,
                        mix_ref, cout_ref, sout_ref, hout_ref, *scr):
    prm = (cw, cb, dtb, alog, dskip, snw, hlb, hnw)
    n_scr = len(scr) // SAMPLE_GROUP
    for g in range(SAMPLE_GROUP):
        pj, mix_s, hst, *chunk_scr = scr[n_scr * g:n_scr * g + n_scr]
        pj[...] = jnp.zeros_like(pj)
        pj[HALO - 3:HALO, X0:X0 + CONV_DIM] = cin_ref[g]
        pj[HALO:HALO + tdec, :] = pj_ref[tdec * g:tdec * g + tdec, :]
        sout_ref[g] = sin_ref[g]
        for hh in range(HGRN_HEADS):
            c = slice(128 * hh, 128 * hh + 128)
            hst[c, :] = hin_ref[g, c, :].T
        _mix_chunk(pj, HALO, SAMPLE_L, tdec, SAMPLE_L, sout_ref.at[g], hst, prm, chunk_scr, mix_s, 0)
        mix_ref[tdec * g:tdec * g + tdec, :] = mix_s[0:tdec, :]
        cout_ref[g] = pj[HALO + tdec - 3:HALO + tdec, X0:X0 + CONV_DIM]
        for hh in range(HGRN_HEADS):
            c = slice(128 * hh, 128 * hh + 128)
            hout_ref[g, c, :] = hst[c, :].T


def _sample_step(proj, conv_in, ssm_in, hgrn_in, consts):
    nseq = conv_in.shape[0]
    tdec = proj.shape[0] // nseq
    assert nseq % SAMPLE_GROUP == 0 and (SAMPLE_GROUP * tdec) % 8 == 0
    seq_spec = lambda shape: pl.BlockSpec((SAMPLE_GROUP,) + shape, lambda i: (i,) + (0,) * len(shape))
    tok_spec = lambda width: pl.BlockSpec((SAMPLE_GROUP * tdec, width), lambda i: (i, 0))
    scratch = ([
        pltpu.VMEM((HALO + SAMPLE_L, PW), F32),
        pltpu.VMEM((SAMPLE_L, 2048), F32),
        pltpu.VMEM((HGRN_HEADS * HGRN_D, HGRN_D), F32),
    ] + _chunk_scratch(SAMPLE_L)) * SAMPLE_GROUP
    return pl.pallas_call(
        functools.partial(_sample_step_kernel, tdec),
        grid=(nseq // SAMPLE_GROUP,),
        in_specs=[tok_spec(PW), seq_spec((CONV_W - 1, CONV_DIM)), seq_spec((SSD_HEADS * SSD_P, SSD_N)),
                  seq_spec((HGRN_HEADS * HGRN_D, HGRN_D))] + [_const_spec(c.shape) for c in consts],
        out_specs=(tok_spec(2048), seq_spec((CONV_W - 1, CONV_DIM)), seq_spec((SSD_HEADS * SSD_P, SSD_N)),
                   seq_spec((HGRN_HEADS * HGRN_D, HGRN_D))),
        out_shape=(jax.ShapeDtypeStruct((nseq * tdec, 2048), F32),
                   jax.ShapeDtypeStruct((nseq, CONV_W - 1, CONV_DIM), F32),
                   jax.ShapeDtypeStruct((nseq, SSD_HEADS * SSD_P, SSD_N), F32),
                   jax.ShapeDtypeStruct((nseq, HGRN_HEADS * HGRN_D, HGRN_D), F32)),
        scratch_shapes=scratch,
        compiler_params=pltpu.CompilerParams(dimension_semantics=("arbitrary",), vmem_limit_bytes=VMEM_LIMIT),
        name="sample_step",
    )(proj, conv_in, ssm_in, hgrn_in, *consts)


def _sample_out_kernel(mix_ref, x_ref, wout_ref, n2_ref, wrt_ref, br_ref, x1_ref, xn2_ref, lg_ref):
    x1, xn2, lgt = _out_part(mix_ref[...].astype(BF16), x_ref[...], wout_ref, n2_ref, wrt_ref, br_ref)
    x1_ref[...] = x1
    _store_rows(xn2_ref, xn2)
    lg_ref[...] = lgt


def _sample_out(mix, x, wout, n2, wrt, br):
    n = x.shape[0]
    return pl.pallas_call(
        _sample_out_kernel,
        out_shape=(jax.ShapeDtypeStruct((n, D_MODEL), F32), jax.ShapeDtypeStruct((n, ROW_TILES, LANE), F32),
                   jax.ShapeDtypeStruct((N_EXPERTS, n), F32)),
        compiler_params=pltpu.CompilerParams(vmem_limit_bytes=VMEM_LIMIT),
        name="sample_out",
    )(mix, x, wout, n2, wrt, br)


ROUTE_TILE = 256
ROUTE_STEP = 2 * ROUTE_TILE


def _route_kernel(n_first, lga_ref, lgb_ref, g_ref, d_ref, cnt_ref, cnt, carry, pstart):
    ph = pl.program_id(0)
    i = pl.program_id(1)
    tl = ROUTE_STEP
    l = jnp.where(i < n_first, lga_ref[...], lgb_ref[...])
    eid = lax.broadcasted_iota(I32, (N_EXPERTS, tl), 0)
    hots, vals = [], []
    for _ in range(TOP_K):
        m = jnp.max(l, axis=0, keepdims=True)
        idx = jnp.min(jnp.where(l == m, eid, N_EXPERTS), axis=0, keepdims=True)
        hot = eid == idx
        hots.append(hot)
        vals.append(m)
        l = jnp.where(hot, -jnp.inf, l)
    ind = hots[0].astype(F32)
    for hot in hots[1:]:
        ind = ind + hot.astype(F32)
    tile_cnt = jnp.sum(ind, axis=1, keepdims=True)

    @pl.when(jnp.logical_and(ph == 0, i == 0))
    def _():
        cnt[...] = jnp.zeros_like(cnt)

    @pl.when(ph == 0)
    def _():
        cnt[...] = cnt[...] + tile_cnt

    @pl.when(jnp.logical_and(ph == 1, i == 0))
    def _():
        padded = jnp.floor((cnt[...] + (MOE_BLOCK - 1)) * (1.0 / MOE_BLOCK)) * MOE_BLOCK
        r = lax.broadcasted_iota(I32, (N_EXPERTS, N_EXPERTS), 0)
        c = lax.broadcasted_iota(I32, (N_EXPERTS, N_EXPERTS), 1)
        pend = jnp.dot((c <= r).astype(F32), padded, precision=HI, preferred_element_type=F32)
        pstart[...] = pend - padded
        carry[...] = jnp.zeros_like(carry)
        cnt_ref[...] = cnt[...]

    @pl.when(ph == 1)
    def _():
        rr = lax.broadcasted_iota(I32, (tl, tl), 0)
        cc = lax.broadcasted_iota(I32, (tl, tl), 1)
        before = _dot(ind.astype(BF16), (rr < cc).astype(BF16))
        base = before + carry[:, 0:1] + pstart[:, 0:1]
        den = jnp.exp(vals[0] - vals[0])
        for k in range(1, TOP_K):
            den = den + jnp.exp(vals[k] - vals[0])
        g_ref[...] = jnp.zeros_like(g_ref)
        for k in range(TOP_K):
            g_ref[k:k + 1, :] = jnp.exp(vals[k] - vals[0]) / den
            dest = jnp.sum(jnp.where(hots[k], base, 0.0), axis=0, keepdims=True).astype(I32)
            for t in range(ROUTE_STEP // ROUTE_TILE):
                d_ref[t, k:k + 1, :] = dest[:, ROUTE_TILE * t:ROUTE_TILE * t + ROUTE_TILE]
        carry[...] = carry[...] + tile_cnt


def _two_source_maps(n_first, trailing=1):
    pad = (0,) * trailing
    return (lambda *idx: (jnp.minimum(idx[-1], n_first - 1),) + pad,
            lambda *idx: (jnp.maximum(idx[-1] - n_first, 0),) + pad)


def _route(lg_a, lg_b):
    ta = lg_a.shape[1] + lg_b.shape[1]
    ns = ta // ROUTE_STEP
    n_first = lg_a.shape[1] // ROUTE_STEP
    per_step = ROUTE_STEP // ROUTE_TILE
    assert n_first * ROUTE_STEP == lg_a.shape[1] and ns * ROUTE_STEP == ta
    return pl.pallas_call(
        functools.partial(_route_kernel, n_first),
        grid=(2, ns),
        in_specs=[pl.BlockSpec((N_EXPERTS, ROUTE_STEP), lambda p, i: (0, jnp.minimum(i, n_first - 1))),
                  pl.BlockSpec((N_EXPERTS, ROUTE_STEP), lambda p, i: (0, jnp.maximum(i - n_first, 0)))],
        out_specs=(pl.BlockSpec((8, ROUTE_STEP), lambda p, i: (0, i * p)),
                   pl.BlockSpec((per_step, TOP_K, ROUTE_TILE), lambda p, i: (i * p, 0, 0)),
                   pl.BlockSpec((N_EXPERTS, LANE), lambda p, i: (0, 0))),
        out_shape=(jax.ShapeDtypeStruct((8, ta), F32),
                   jax.ShapeDtypeStruct((ta // ROUTE_TILE, TOP_K, ROUTE_TILE), I32),
                   jax.ShapeDtypeStruct((N_EXPERTS, LANE), F32)),
        scratch_shapes=[pltpu.VMEM((N_EXPERTS, LANE), F32)] * 3,
        compiler_params=pltpu.CompilerParams(dimension_semantics=("arbitrary", "arbitrary")),
        name="route",
    )(lg_a, lg_b)


DMA_UNROLL = 8


def _row_copy(src, dst, sem):
    return pltpu.make_async_copy(src, dst, sem)


def _dispatch_kernel(n_first, pend_ref, padded_ref, nu_ref, d_ref, xa_ref, xb_ref, o_ref, zeros, sem):
    i = pl.program_id(0)
    tl = ROUTE_TILE
    nb = o_ref.shape[0] // MOE_BLOCK

    @pl.when(i == 0)
    def _():
        zeros[...] = jnp.zeros_like(zeros)

        def fills(e):
            return ((padded_ref[e] > 0, pend_ref[e] - MOE_BLOCK),
                    (nu_ref[0] + e < nb, (nu_ref[0] + e) * MOE_BLOCK))

        for e in range(N_EXPERTS):
            for cond, row in fills(e):
                @pl.when(cond)
                def _(row=row):
                    _row_copy(zeros, o_ref.at[pl.ds(row, MOE_BLOCK)], sem).start()
        for e in range(N_EXPERTS):
            for cond, row in fills(e):
                @pl.when(cond)
                def _(row=row):
                    _row_copy(zeros, o_ref.at[pl.ds(row, MOE_BLOCK)], sem).wait()

    def scatter_rows(x_ref):
        def start(j2, c):
            for u in range(DMA_UNROLL):
                j = j2 * DMA_UNROLL + u
                for k in range(TOP_K):
                    _row_copy(x_ref.at[j], o_ref.at[d_ref[k, j]], sem).start(priority=(u + k) % 2)
            return c

        lax.fori_loop(0, tl // DMA_UNROLL, start, 0)

        def wait(j2, c):
            for u in range(DMA_UNROLL):
                j = j2 * DMA_UNROLL + u
                for k in range(TOP_K):
                    _row_copy(x_ref.at[j], o_ref.at[d_ref[k, j]], sem).wait()
            return c

        lax.fori_loop(0, tl // DMA_UNROLL, wait, 0)

    @pl.when(i < n_first)
    def _():
        scatter_rows(xa_ref)

    @pl.when(i >= n_first)
    def _():
        scatter_rows(xb_ref)


def _dispatch(xa, xb, d3, pend, padded, n_used, n_rows):
    nt = (xa.shape[0] + xb.shape[0]) // ROUTE_TILE
    n_first = xa.shape[0] // ROUTE_TILE
    map_a, map_b = _two_source_maps(n_first, trailing=2)
    grid_spec = pltpu.PrefetchScalarGridSpec(
        num_scalar_prefetch=3,
        grid=(nt,),
        in_specs=[pl.BlockSpec((None, TOP_K, ROUTE_TILE), lambda i, *_: (i, 0, 0), memory_space=pltpu.SMEM),
                  pl.BlockSpec((ROUTE_TILE, ROW_TILES, LANE), lambda i, *_: map_a(i)),
                  pl.BlockSpec((ROUTE_TILE, ROW_TILES, LANE), lambda i, *_: map_b(i))],
        out_specs=pl.BlockSpec(memory_space=pl.ANY),
        scratch_shapes=[pltpu.VMEM((MOE_BLOCK, ROW_TILES, LANE), F32), pltpu.SemaphoreType.DMA],
    )
    return pl.pallas_call(
        functools.partial(_dispatch_kernel, n_first),
        grid_spec=grid_spec,
        out_shape=jax.ShapeDtypeStruct((n_rows, ROW_TILES, LANE), F32),
        compiler_params=pltpu.CompilerParams(dimension_semantics=("arbitrary",)),
        name="dispatch",
    )(pend, padded, n_used, d3, xa, xb)


def _expert_kernel(be_ref, nu_ref, x_ref, wg_ref, bg_ref, wu_ref, bu_ref, wd_ref, bd_ref, o_ref, wg_s, wu_s, wd_s,
                   lhs_s):
    i = pl.program_id(0)
    prev = be_ref[jnp.maximum(i - 1, 0)]
    fresh = jnp.logical_or(i == 0, be_ref[i] != prev)

    @pl.when(jnp.logical_and(i < nu_ref[0], fresh))
    def _():
        wg_s[...] = wg_ref[0].astype(BF16)
        wu_s[...] = wu_ref[0].astype(BF16)
        wd_s[...] = wd_ref[0].astype(BF16)

    @pl.when(i < nu_ref[0])
    def _():
        for s, slab in enumerate(_load_rows(x_ref)):
            lhs_s[:, LANE * s:LANE * s + LANE] = slab.astype(BF16)
        x = lhs_s[...]
        g = _dot(x, wg_s[...]) + bg_ref[0]
        u = _dot(x, wu_s[...]) + bu_ref[0]
        g = jnp.minimum(g, SWIGLU_LIMIT)
        u = jnp.clip(u, -SWIGLU_LIMIT, SWIGLU_LIMIT)
        act = g * jax.nn.sigmoid(SWIGLU_ALPHA * g) * (u + 1.0)
        _store_rows(o_ref, _dot(act.astype(BF16), wd_s[...]) + bd_ref[0])

    @pl.when(i >= nu_ref[0])
    def _():
        o_ref[...] = jnp.zeros_like(o_ref)


def _experts(xs, block_e, n_used, wg, bg, wu, bu, wd, bd):
    n_rows = xs.shape[0]
    nb = n_rows // MOE_BLOCK
    row_map = lambda i, be, nu: (jnp.maximum(jnp.minimum(i, nu[0] - 1), 0), 0, 0)
    w_map = lambda i, be, nu: (be[i], 0, 0)
    w_spec = pl.BlockSpec((1, D_MODEL, D_MODEL), w_map)
    b_spec = pl.BlockSpec((1, 1, D_MODEL), w_map)
    grid_spec = pltpu.PrefetchScalarGridSpec(
        num_scalar_prefetch=2,
        grid=(nb,),
        in_specs=[pl.BlockSpec((MOE_BLOCK, ROW_TILES, LANE), row_map), w_spec, b_spec, w_spec, b_spec, w_spec,
                  b_spec],
        out_specs=pl.BlockSpec((MOE_BLOCK, ROW_TILES, LANE), lambda i, be, nu: (i, 0, 0)),
        scratch_shapes=[pltpu.VMEM((D_MODEL, D_MODEL), BF16)] * 3 + [pltpu.VMEM((MOE_BLOCK, D_MODEL), BF16)],
    )
    return pl.pallas_call(
        _expert_kernel,
        grid_spec=grid_spec,
        out_shape=jax.ShapeDtypeStruct((n_rows, ROW_TILES, LANE), F32),
        compiler_params=pltpu.CompilerParams(dimension_semantics=("arbitrary",), vmem_limit_bytes=VMEM_LIMIT),
        name="experts",
    )(block_e, n_used, xs, wg, bg.reshape(N_EXPERTS, 1, D_MODEL), wu, bu.reshape(N_EXPERTS, 1, D_MODEL), wd,
      bd.reshape(N_EXPERTS, 1, D_MODEL))


def _combine_kernel(n_first, d_ref, x1a_ref, x1b_ref, g_ref, nf_ref, eo_ref, ya_ref, yb_ref, buf, y_s, sem):
    i = pl.program_id(0)
    tl = ROUTE_TILE

    def gather(start):
        def body(j2, c):
            for u in range(DMA_UNROLL):
                j = j2 * DMA_UNROLL + u
                for k in range(TOP_K):
                    cp = _row_copy(eo_ref.at[d_ref[k, j]], buf.at[k, j], sem)
                    if start:
                        cp.start(priority=(u + k) % 2)
                    else:
                        cp.wait()
            return c

        lax.fori_loop(0, tl // DMA_UNROLL, body, 0)

    gather(True)
    sel = (lax.broadcasted_iota(I32, (8, LANE), 0) == lax.broadcasted_iota(I32, (8, LANE), 1)).astype(F32)
    gt = _dot_tn(g_ref[...], sel, precision=HI)
    gather(False)
    for k in range(TOP_K):
        for s, slab in enumerate(_load_rows(buf.at[k])):
            cols = slice(LANE * s, LANE * s + LANE)
            term = gt[:, k:k + 1] * slab
            y_s[:, cols] = term if k == 0 else y_s[:, cols] + term
    y = jnp.where(i < n_first, x1a_ref[...], x1b_ref[...]) + y_s[...]
    y = y * lax.rsqrt(jnp.mean(y * y, axis=-1, keepdims=True) + EPS) * nf_ref[...]

    @pl.when(i < n_first)
    def _():
        ya_ref[...] = y

    @pl.when(i >= n_first)
    def _():
        yb_ref[...] = y


def _combine(x1a, x1b, gates8, d3, eo3, nf):
    nt = (x1a.shape[0] + x1b.shape[0]) // ROUTE_TILE
    n_first = x1a.shape[0] // ROUTE_TILE
    map_a, map_b = _two_source_maps(n_first)
    row_specs = (pl.BlockSpec((ROUTE_TILE, D_MODEL), map_a), pl.BlockSpec((ROUTE_TILE, D_MODEL), map_b))
    grid_spec = pltpu.PrefetchScalarGridSpec(
        num_scalar_prefetch=0,
        grid=(nt,),
        in_specs=[pl.BlockSpec((None, TOP_K, ROUTE_TILE), lambda i: (i, 0, 0), memory_space=pltpu.SMEM),
                  *row_specs,
                  pl.BlockSpec((8, ROUTE_TILE), lambda i: (0, i)),
                  _const_spec(nf.shape),
                  pl.BlockSpec(memory_space=pl.ANY)],
        out_specs=row_specs,
        scratch_shapes=[pltpu.VMEM((TOP_K, ROUTE_TILE, ROW_TILES, LANE), F32),
                        pltpu.VMEM((ROUTE_TILE, D_MODEL), F32), pltpu.SemaphoreType.DMA],
    )
    return pl.pallas_call(
        functools.partial(_combine_kernel, n_first),
        grid_spec=grid_spec,
        out_shape=(jax.ShapeDtypeStruct(x1a.shape, F32), jax.ShapeDtypeStruct(x1b.shape, F32)),
        compiler_params=pltpu.CompilerParams(dimension_semantics=("arbitrary",), vmem_limit_bytes=VMEM_LIMIT),
        name="combine",
    )(d3, x1a, x1b, gates8, nf, eo3)


def _moe_and_norm(x1, xn2, lgt, wg, bg, wu, bu, wd, bd, nf):
    ta = x1[0].shape[0] + x1[1].shape[0]
    n_rows = (-(-(ta * TOP_K) // MOE_BLOCK) + N_EXPERTS) * MOE_BLOCK
    gates8, d3, cnt = _route(*lgt)
    counts = cnt[:, 0].astype(I32)
    padded = (counts + MOE_BLOCK - 1) // MOE_BLOCK * MOE_BLOCK
    pend = jnp.cumsum(padded)
    n_used = (pend[-1:] // MOE_BLOCK).astype(I32)
    block_row = jnp.arange(n_rows // MOE_BLOCK, dtype=I32) * MOE_BLOCK
    block_e = jnp.minimum(jnp.sum((pend[None, :] <= block_row[:, None]).astype(I32), axis=1), N_EXPERTS - 1)
    xs = _dispatch(*xn2, d3, pend.astype(I32), padded.astype(I32), n_used, n_rows)
    eo = _experts(xs, block_e, n_used, wg, bg, wu, bu, wd, bd)
    return _combine(*x1, gates8, d3, eo, nf)


def _prep_consts(norm1_w, w_in, conv_w, conv_b, dt_bias, A_log, D_skip, ssd_norm_w, hgrn_lower_bound, hgrn_norm_w,
                 w_out, norm2_w, w_router, b_router):
    row = lambda v: v.reshape(1, -1).astype(F32)
    pad_lane = lambda v: jnp.pad(v.reshape(1, -1).astype(F32), ((0, 0), (0, LANE - v.size)))
    wb = w_in.astype(BF16)
    win = jnp.concatenate([wb[:, :2560], wb[:, 2576:6672], wb[:, 2560:2576],
                           jnp.zeros((D_MODEL, LANE - SSD_HEADS), BF16)], axis=1)
    dskip = jnp.repeat(D_skip.astype(F32), SSD_P).reshape(1, -1)
    return (row(norm1_w), win, conv_w.astype(F32), row(conv_b), pad_lane(dt_bias), pad_lane(A_log), dskip,
            row(ssd_norm_w), hgrn_lower_bound.astype(F32), row(hgrn_norm_w), w_out.astype(BF16), row(norm2_w),
            w_router.T.astype(F32), b_router.reshape(-1, 1).astype(F32))


def kernel(x_prompt, x_sample, state_conv, state_ssm, state_hgrn, norm1_w, w_in, conv_w, conv_b, dt_bias, A_log, D_skip, ssd_norm_w, hgrn_lower_bound, hgrn_norm_w, w_out, norm2_w, w_router, b_router, w_gate, b_gate, w_up, b_up, w_down, b_down, norm_f_w):
    depth = w_in.shape[0]
    assert depth == 1
    bsz, seq, _ = x_prompt.shape
    nseq, tdec, _ = x_sample.shape
    consts = _prep_consts(norm1_w[0], w_in[0], conv_w[0], conv_b[0], dt_bias[0], A_log[0], D_skip[0], ssd_norm_w[0],
                          hgrn_lower_bound, hgrn_norm_w[0], w_out[0], norm2_w[0], w_router[0], b_router[0])
    n1, win, cw, cb, dtb, alog, dskip, snw, hlb, hnw, wout, n2, wrt, br = consts
    chunk_consts = (cw, cb, dtb, alog, dskip, snw, hlb, hnw)

    x1_p, xn2_p, lg_p, conv_p, ssm_p, hgrn_p = _prompt_mixer(x_prompt, consts)

    xs_flat = x_sample.reshape(nseq * tdec, D_MODEL)
    proj_s = _sample_inproj(xs_flat, n1, win)
    mix_s, conv_s, ssm_s, hgrn_s = _sample_step(
        proj_s, state_conv[0], state_ssm[0].reshape(nseq, SSD_HEADS * SSD_P, SSD_N),
        state_hgrn[0].reshape(nseq, HGRN_HEADS * HGRN_D, HGRN_D), chunk_consts)
    x1_s, xn2_s, lg_s = _sample_out(mix_s, xs_flat, wout, n2, wrt, br)

    y_p, y_s = _moe_and_norm((x1_p, x1_s), (xn2_p, xn2_s), (lg_p, lg_s), w_gate[0], b_gate[0], w_up[0], b_up[0],
                             w_down[0], b_down[0], norm_f_w.reshape(1, -1).astype(F32))
    y_prompt = y_p.reshape(bsz, seq, D_MODEL)
    y_sample = y_s.reshape(nseq, tdec, D_MODEL)
    return (y_prompt, y_sample,
            conv_p[None], ssm_p.reshape(1, bsz, SSD_HEADS, SSD_P, SSD_N),
            hgrn_p.reshape(1, bsz, HGRN_HEADS, HGRN_D, HGRN_D),
            conv_s[None], ssm_s.reshape(1, nseq, SSD_HEADS, SSD_P, SSD_N),
            hgrn_s.reshape(1, nseq, HGRN_HEADS, HGRN_D, HGRN_D))
```

```python
import functools

import jax
import jax.numpy as jnp
from jax import lax
from jax.experimental import pallas as pl
from jax.experimental.pallas import tpu as pltpu

F32 = jnp.float32
BF16 = jnp.bfloat16
I32 = jnp.int32
HI = lax.Precision.HIGHEST

D_MODEL = 1024
SSD_HEADS = 16
SSD_P = 64
SSD_N = 128
CONV_DIM = 1536
CONV_W = 4
HGRN_HEADS = 8
HGRN_D = 128
N_EXPERTS = 32
TOP_K = 4
MOE_BLOCK = 512
SWIGLU_LIMIT = 7.0
SWIGLU_ALPHA = 1.702
EPS = 1e-6

Z0 = 0
X0 = 1024
Q0 = 2560
F0 = 3584
I0 = 4608
G0 = 5632
DT0 = 6656
PW = 6784
LANE = 128
HALO = 8
SSD_CHUNK = 128
HGRN_CHUNK = 16
MAX_SUB = SSD_CHUNK // HGRN_CHUNK
VMEM_LIMIT = 56 * 1024 * 1024


def _silu(x):
    return x * jax.nn.sigmoid(x)


def _softplus(x):
    return jnp.maximum(x, 0.0) + jnp.log1p(jnp.exp(-jnp.abs(x)))


def _dot(a, b):
    return jnp.dot(a, b, preferred_element_type=F32)


def _dot_nt(a, b, precision=None):
    return lax.dot_general(a, b, (((1,), (1,)), ((), ())), precision=precision, preferred_element_type=F32)


def _dot_tn(a, b, precision=None):
    return lax.dot_general(a, b, (((0,), (0,)), ((), ())), precision=precision, preferred_element_type=F32)


def _split3(x):
    hi = x.astype(BF16)
    r1 = x - hi.astype(F32)
    mid = r1.astype(BF16)
    lo = (r1 - mid.astype(F32)).astype(BF16)
    return hi, mid, lo


def _sel_dot(sel, x):
    return sum(_dot(sel, p) for p in _split3(x))


def _sel_dot_tn(x, sel):
    return sum(_dot_tn(p, sel) for p in _split3(x))


WIN_PARTS = (Z0, Q0, DT0)


def _in_proj(x, n1_ref, win_refs, pj, r0):
    rows = x.shape[0]
    ms = jnp.mean(x * x, axis=-1, keepdims=True)
    h = (x * lax.rsqrt(ms + EPS) * n1_ref[...]).astype(BF16)
    for w_ref, base in zip(win_refs, WIN_PARTS):
        width = w_ref.shape[1]
        for c0 in range(0, width, 512):
            c1 = min(c0 + 512, width)
            pj[r0:r0 + rows, base + c0:base + c1] = _dot(h, w_ref[:, c0:c1])


def _mix_chunk(pj, r0, L, nv, LH, ssm, hst, prm, scr, mix_s, m0):
    cw, cb, dtb, alog, dskip, snw, hlb, hnw = prm
    y_s, b_s, kk_s, r_s, q_s, qb_s, v_s, kend_s = scr

    def conv_cols(c0, c1):
        acc = cb[:, c0:c1]
        for k in range(CONV_W):
            acc = acc + pj[r0 - 3 + k:r0 - 3 + k + L, X0 + c0:X0 + c1] * cw[k:k + 1, c0:c1]
        return _silu(acc)

    rowi = lax.broadcasted_iota(I32, (L, L), 0)
    coli = lax.broadcasted_iota(I32, (L, L), 1)
    causal = coli <= rowi
    rid = lax.broadcasted_iota(I32, (L, LANE), 0)
    lane = lax.broadcasted_iota(I32, (L, LANE), 1)
    lo = lane < SSD_P

    dt = _softplus(pj[r0:r0 + L, DT0:DT0 + LANE] + dtb[...])
    if nv < L:
        dt = jnp.where(rid < nv, dt, 0.0)
    a = dt * (-jnp.exp(alog[...]))
    acum = _sel_dot(causal.astype(BF16), a)
    acum_t = _sel_dot_tn(a, (rowi <= coli).astype(BF16))
    dt_t = _sel_dot_tn(dt, (rowi == coli).astype(BF16))
    eac = jnp.exp(acum)
    last = acum[L - 1:L, :]
    elast = jnp.exp(last)
    tail = jnp.exp(last - acum) * dt

    bm = [conv_cols(1024 + 128 * g, 1152 + 128 * g).astype(BF16) for g in range(2)]
    cm = [conv_cols(1280 + 128 * g, 1408 + 128 * g).astype(BF16) for g in range(2)]
    gcb = [_dot_nt(cm[g], bm[g]) for g in range(2)]
    rowh = lax.broadcasted_iota(I32, (LANE, LANE), 0) < SSD_P

    for j in range(SSD_HEADS // 2):
        g = j // 4
        cols = slice(128 * j, 128 * j + 128)
        xs = conv_cols(128 * j, 128 * j + 128)
        xsb = xs.astype(BF16)
        ys = []
        for h in (2 * j, 2 * j + 1):
            seg = acum[:, h:h + 1] - acum_t[h:h + 1, :]
            dec = jnp.exp(jnp.where(causal, seg, -jnp.inf))
            mh = (gcb[g] * dec * dt_t[h:h + 1, :]).astype(BF16)
            ys.append(_dot(mh, xsb))
        y = jnp.where(lo, ys[0], ys[1])
        hp = ssm[128 * j:128 * j + 128, :]
        yoff = _dot_nt(cm[g], hp.astype(BF16))
        y = y + yoff * jnp.where(lo, eac[:, 2 * j:2 * j + 1], eac[:, 2 * j + 1:2 * j + 2])
        y = y + dskip[:, cols] * xs
        y_s[0:L, cols] = y
        xt = (xs * jnp.where(lo, tail[:, 2 * j:2 * j + 1], tail[:, 2 * j + 1:2 * j + 2])).astype(BF16)
        upd = _dot_tn(xt, bm[g])
        el = jnp.where(rowh, elast[:, 2 * j:2 * j + 1], elast[:, 2 * j + 1:2 * j + 2])
        ssm[128 * j:128 * j + 128, :] = el * hp + upd

    yz = y_s[0:L, :] * _silu(pj[r0:r0 + L, Z0:Z0 + 1024])
    for g in range(2):
        sl = slice(512 * g, 512 * g + 512)
        part = yz[:, sl]
        ms = jnp.mean(part * part, axis=-1, keepdims=True)
        mix_s[m0:m0 + L, sl] = (part * lax.rsqrt(ms + EPS) * snw[:, sl]).astype(mix_s.dtype)

    n_sub = L // LH
    AH = min(L, 2 * LH)
    n_att = L // AH
    r0v = hlb[0:1, :]
    r1v = hlb[1:2, :]
    mx = jnp.maximum(r0v, r1v)
    e0 = jnp.exp(r0v - mx)
    lb = e0 / (e0 + jnp.exp(r1v - mx))
    rid_w = lax.broadcasted_iota(I32, (L, 1024), 0)
    f = lb + (1.0 - lb) * jax.nn.sigmoid(pj[r0:r0 + L, F0:F0 + 1024])
    if nv < L:
        f = jnp.where(rid_w < nv, f, 1.0)
    kk = 1.0 - f
    blk = jnp.logical_and(causal, (rowi // LH) == (coli // LH)).astype(F32)
    b = _sel_dot(blk.astype(BF16), jnp.log(f))
    q = _silu(pj[r0:r0 + L, Q0:Q0 + 1024])
    base = jnp.zeros((1, 1024), F32)
    for i in range(n_sub):
        rows = slice(LH * i, LH * i + LH)
        bi = b[rows, :] + base
        b_s[rows, :] = bi
        qb_s[rows, :] = (q[rows, :] * jnp.exp(bi)).astype(BF16)
        base = bi[LH - 1:LH, :]
    for i in range(n_att):
        rows = slice(AH * i, AH * i + AH)
        mid = b_s[AH * i + AH // 2 - 1:AH * i + AH // 2, :]
        r_s[i:i + 1, :] = mid
        q_s[rows, :] = (q[rows, :] * jnp.exp(b_s[rows, :] - mid)).astype(BF16)
    kk_s[0:L, :] = kk
    v_s[0:L, :] = pj[r0:r0 + L, I0:I0 + 1024].astype(BF16)
    kend_s[0:L, :] = (kk * jnp.exp(base - b_s[0:L, :])).astype(BF16)
    ebl = jnp.exp(base)
    colq = lax.broadcasted_iota(I32, (AH, L), 1)
    rowq = lax.broadcasted_iota(I32, (AH, L), 0)

    for hh in range(HGRN_HEADS):
        c = slice(128 * hh, 128 * hh + 128)
        bh = b_s[0:L, c]
        kkh = kk_s[0:L, c]
        parts = []
        for i in range(n_att):
            n_keys = AH * i + AH
            ktil = (kkh[0:n_keys, :] * jnp.exp(r_s[i:i + 1, c] - bh[0:n_keys, :])).astype(BF16)
            if n_keys < L:
                ktil = jnp.concatenate([ktil, jnp.zeros((L - n_keys, LANE), BF16)], axis=0)
            att = _dot_nt(q_s[AH * i:AH * i + AH, c], ktil)
            parts.append(jnp.where(colq <= rowq + AH * i, att, 0.0).astype(BF16))
        att_all = parts[0] if n_att == 1 else jnp.concatenate(parts, axis=0)
        st = hst[c, :]
        oh = _dot(att_all, v_s[0:L, c]) + _dot_nt(qb_s[0:L, c], st.astype(BF16))
        hst[c, :] = ebl[:, c] * st + _dot_tn(v_s[0:L, c], kend_s[0:L, c])
        ms = jnp.mean(oh * oh, axis=-1, keepdims=True)
        on = oh * lax.rsqrt(ms + EPS) * hnw[:, c]
        mix_s[m0:m0 + L, 1024 + 128 * hh:1152 + 128 * hh] = (
            on * _silu(pj[r0:r0 + L, G0 + 128 * hh:G0 + 128 * hh + 128])).astype(mix_s.dtype)


def _chunk_scratch(rows):
    wide = lambda dt: pltpu.VMEM((rows, 1024), dt)
    return [wide(F32), wide(F32), wide(F32), pltpu.VMEM((MAX_SUB, 1024), F32), wide(BF16), wide(BF16), wide(BF16),
            wide(BF16)]


ROW_TILES = D_MODEL // LANE


def _store_rows(ref, val):
    for s in range(ROW_TILES):
        ref[:, s, :] = val[:, LANE * s:LANE * s + LANE]


def _load_rows(ref):
    planes = jnp.swapaxes(ref[...], 0, 1)
    return [planes[s] for s in range(ROW_TILES)]


def _out_part(mix, x, wout_ref, n2_ref, wrt_ref, br_ref):
    x1 = x + _dot(mix, wout_ref[...])
    ms = jnp.mean(x1 * x1, axis=-1, keepdims=True)
    xn2 = x1 * lax.rsqrt(ms + EPS) * n2_ref[...]
    lgt = _dot_nt(wrt_ref[...], xn2, precision=HI) + br_ref[...]
    return x1, xn2, lgt


def _prompt_kernel(tt, nt, x_ref, n1_ref, wa_ref, wb_ref, wdt_ref, cw, cb, dtb, alog, dskip, snw, hlb, hnw, wout_ref,
                   n2_ref, wrt_ref, br_ref, x1_ref, xn2_ref, lg_ref, conv_ref, ssm_ref, hgrn_ref,
                   pj, mix_s, hst, *scr):
    t = pl.program_id(1)

    @pl.when(t == 0)
    def _():
        pj[0:HALO, :] = jnp.zeros((HALO, PW), F32)
        ssm_ref[0] = jnp.zeros((SSD_HEADS * SSD_P, SSD_N), F32)
        hst[...] = jnp.zeros_like(hst)

    x = x_ref[0]
    _in_proj(x, n1_ref, (wa_ref, wb_ref, wdt_ref), pj, HALO)
    prm = (cw, cb, dtb, alog, dskip, snw, hlb, hnw)
    n_scr = len(scr) // (tt // SSD_CHUNK)
    for c in range(tt // SSD_CHUNK):
        _mix_chunk(pj, HALO + SSD_CHUNK * c, SSD_CHUNK, SSD_CHUNK, HGRN_CHUNK, ssm_ref.at[0], hst, prm,
                   scr[n_scr * c:n_scr * c + n_scr], mix_s, SSD_CHUNK * c)
    x1, xn2, lgt = _out_part(mix_s[...], x, wout_ref, n2_ref, wrt_ref, br_ref)
    x1_ref[...] = x1
    _store_rows(xn2_ref, xn2)
    lg_ref[...] = lgt

    @pl.when(t == nt - 1)
    def _():
        conv_ref[0] = pj[HALO + tt - 3:HALO + tt, X0:X0 + CONV_DIM]
        for hh in range(HGRN_HEADS):
            c = slice(128 * hh, 128 * hh + 128)
            hgrn_ref[0, c, :] = hst[c, :].T

    pj[0:HALO, X0:X0 + CONV_DIM] = pj[tt:tt + HALO, X0:X0 + CONV_DIM]


def _const_spec(shape):
    nd = len(shape)
    return pl.BlockSpec(shape, lambda *_: (0,) * nd, pipeline_mode=pl.Buffered(1))


def _prompt_mixer(x, consts, tt=256):
    bsz, seq, _ = x.shape
    nt = seq // tt
    ta = bsz * seq
    const_specs = [_const_spec(c.shape) for c in consts]
    out_shape = (
        jax.ShapeDtypeStruct((ta, D_MODEL), F32),
        jax.ShapeDtypeStruct((ta, ROW_TILES, LANE), F32),
        jax.ShapeDtypeStruct((N_EXPERTS, ta), F32),
        jax.ShapeDtypeStruct((bsz, CONV_W - 1, CONV_DIM), F32),
        jax.ShapeDtypeStruct((bsz, SSD_HEADS * SSD_P, SSD_N), F32),
        jax.ShapeDtypeStruct((bsz, HGRN_HEADS * HGRN_D, HGRN_D), F32),
    )
    out_specs = (
        pl.BlockSpec((tt, D_MODEL), lambda b, t: (b * nt + t, 0)),
        pl.BlockSpec((tt, ROW_TILES, LANE), lambda b, t: (b * nt + t, 0, 0)),
        pl.BlockSpec((N_EXPERTS, tt), lambda b, t: (0, b * nt + t)),
        pl.BlockSpec((1, CONV_W - 1, CONV_DIM), lambda b, t: (b, 0, 0)),
        pl.BlockSpec((1, SSD_HEADS * SSD_P, SSD_N), lambda b, t: (b, 0, 0)),
        pl.BlockSpec((1, HGRN_HEADS * HGRN_D, HGRN_D), lambda b, t: (b, 0, 0)),
    )
    scratch = [
        pltpu.VMEM((HALO + tt, PW), F32),
        pltpu.VMEM((tt, 2048), BF16),
        pltpu.VMEM((HGRN_HEADS * HGRN_D, HGRN_D), F32),
    ] + _chunk_scratch(SSD_CHUNK) * (tt // SSD_CHUNK)
    return pl.pallas_call(
        functools.partial(_prompt_kernel, tt, nt),
        grid=(bsz, nt),
        in_specs=[pl.BlockSpec((1, tt, D_MODEL), lambda b, t: (b, t, 0))] + const_specs,
        out_specs=out_specs,
        out_shape=out_shape,
        scratch_shapes=scratch,
        compiler_params=pltpu.CompilerParams(dimension_semantics=("arbitrary", "arbitrary"),
                                             vmem_limit_bytes=VMEM_LIMIT),
        name="prompt_mixer",
    )(x, *consts)


def _sample_inproj_kernel(x_ref, n1_ref, wa_ref, wb_ref, wdt_ref, o_ref):
    _in_proj(x_ref[...], n1_ref, (wa_ref, wb_ref, wdt_ref), o_ref, 0)


def _sample_inproj(x, n1, win, rows=256):
    n = x.shape[0]
    return pl.pallas_call(
        _sample_inproj_kernel,
        grid=(n // rows,),
        in_specs=[pl.BlockSpec((rows, D_MODEL), lambda i: (i, 0)), _const_spec(n1.shape)]
        + [_const_spec(w.shape) for w in win],
        out_specs=pl.BlockSpec((rows, PW), lambda i: (i, 0)),
        out_shape=jax.ShapeDtypeStruct((n, PW), F32),
        compiler_params=pltpu.CompilerParams(dimension_semantics=("arbitrary",), vmem_limit_bytes=VMEM_LIMIT),
        name="sample_inproj",
    )(x, n1, *win)


SAMPLE_L = 16


SAMPLE_GROUP = 4


def _sample_step_kernel(tdec, pj_ref, cin_ref, sin_ref, hin_ref, cw, cb, dtb, alog, dskip, snw, hlb, hnw,
                        mix_ref, cout_ref, sout_ref, hout_ref, *scr):
    prm = (cw, cb, dtb, alog, dskip, snw, hlb, hnw)
    n_scr = len(scr) // SAMPLE_GROUP
    for g in range(SAMPLE_GROUP):
        pj, mix_s, hst, *chunk_scr = scr[n_scr * g:n_scr * g + n_scr]
        pj[HALO + tdec:HALO + SAMPLE_L, :] = jnp.zeros((SAMPLE_L - tdec, PW), F32)
        pj[HALO - 3:HALO, X0:X0 + CONV_DIM] = cin_ref[g]
        pj[HALO:HALO + tdec, :] = pj_ref[tdec * g:tdec * g + tdec, :]
        sout_ref[g] = sin_ref[g]
        for hh in range(HGRN_HEADS):
            c = slice(128 * hh, 128 * hh + 128)
            hst[c, :] = hin_ref[g, c, :].T
        _mix_chunk(pj, HALO, SAMPLE_L, tdec, SAMPLE_L, sout_ref.at[g], hst, prm, chunk_scr, mix_s, 0)
        mix_ref[tdec * g:tdec * g + tdec, :] = mix_s[0:tdec, :]
        cout_ref[g] = pj[HALO + tdec - 3:HALO + tdec, X0:X0 + CONV_DIM]
        for hh in range(HGRN_HEADS):
            c = slice(128 * hh, 128 * hh + 128)
            hout_ref[g, c, :] = hst[c, :].T


def _sample_step(proj, conv_in, ssm_in, hgrn_in, consts):
    nseq = conv_in.shape[0]
    tdec = proj.shape[0] // nseq
    assert nseq % SAMPLE_GROUP == 0 and (SAMPLE_GROUP * tdec) % 8 == 0
    seq_spec = lambda shape: pl.BlockSpec((SAMPLE_GROUP,) + shape, lambda i: (i,) + (0,) * len(shape))
    tok_spec = lambda width: pl.BlockSpec((SAMPLE_GROUP * tdec, width), lambda i: (i, 0))
    scratch = ([
        pltpu.VMEM((HALO + SAMPLE_L, PW), F32),
        pltpu.VMEM((SAMPLE_L, 2048), F32),
        pltpu.VMEM((HGRN_HEADS * HGRN_D, HGRN_D), F32),
    ] + _chunk_scratch(SAMPLE_L)) * SAMPLE_GROUP
    return pl.pallas_call(
        functools.partial(_sample_step_kernel, tdec),
        grid=(nseq // SAMPLE_GROUP,),
        in_specs=[tok_spec(PW), seq_spec((CONV_W - 1, CONV_DIM)), seq_spec((SSD_HEADS * SSD_P, SSD_N)),
                  seq_spec((HGRN_HEADS * HGRN_D, HGRN_D))] + [_const_spec(c.shape) for c in consts],
        out_specs=(tok_spec(2048), seq_spec((CONV_W - 1, CONV_DIM)), seq_spec((SSD_HEADS * SSD_P, SSD_N)),
                   seq_spec((HGRN_HEADS * HGRN_D, HGRN_D))),
        out_shape=(jax.ShapeDtypeStruct((nseq * tdec, 2048), F32),
                   jax.ShapeDtypeStruct((nseq, CONV_W - 1, CONV_DIM), F32),
                   jax.ShapeDtypeStruct((nseq, SSD_HEADS * SSD_P, SSD_N), F32),
                   jax.ShapeDtypeStruct((nseq, HGRN_HEADS * HGRN_D, HGRN_D), F32)),
        scratch_shapes=scratch,
        compiler_params=pltpu.CompilerParams(dimension_semantics=("arbitrary",), vmem_limit_bytes=VMEM_LIMIT),
        name="sample_step",
    )(proj, conv_in, ssm_in, hgrn_in, *consts)


def _sample_out_kernel(mix_ref, x_ref, wout_ref, n2_ref, wrt_ref, br_ref, x1_ref, xn2_ref, lg_ref):
    x1, xn2, lgt = _out_part(mix_ref[...].astype(BF16), x_ref[...], wout_ref, n2_ref, wrt_ref, br_ref)
    x1_ref[...] = x1
    _store_rows(xn2_ref, xn2)
    lg_ref[...] = lgt


def _sample_out(mix, x, wout, n2, wrt, br):
    n = x.shape[0]
    return pl.pallas_call(
        _sample_out_kernel,
        out_shape=(jax.ShapeDtypeStruct((n, D_MODEL), F32), jax.ShapeDtypeStruct((n, ROW_TILES, LANE), F32),
                   jax.ShapeDtypeStruct((N_EXPERTS, n), F32)),
        compiler_params=pltpu.CompilerParams(vmem_limit_bytes=VMEM_LIMIT),
        name="sample_out",
    )(mix, x, wout, n2, wrt, br)


ROUTE_TILE = 256
ROUTE_STEP = 2 * ROUTE_TILE


def _route_kernel(n_first, lga_ref, lgb_ref, g_ref, d_ref, cnt_ref, cnt, carry, pstart):
    ph = pl.program_id(0)
    i = pl.program_id(1)
    tl = ROUTE_STEP
    l = jnp.where(i < n_first, lga_ref[...], lgb_ref[...])
    eid = lax.broadcasted_iota(I32, (N_EXPERTS, tl), 0)
    hots, vals = [], []
    for _ in range(TOP_K):
        m = jnp.max(l, axis=0, keepdims=True)
        idx = jnp.min(jnp.where(l == m, eid, N_EXPERTS), axis=0, keepdims=True)
        hot = eid == idx
        hots.append(hot)
        vals.append(m)
        l = jnp.where(hot, -jnp.inf, l)
    ind = hots[0].astype(F32)
    for hot in hots[1:]:
        ind = ind + hot.astype(F32)
    tile_cnt = jnp.sum(ind, axis=1, keepdims=True)

    @pl.when(jnp.logical_and(ph == 0, i == 0))
    def _():
        cnt[...] = jnp.zeros_like(cnt)

    @pl.when(ph == 0)
    def _():
        cnt[...] = cnt[...] + tile_cnt

    @pl.when(jnp.logical_and(ph == 1, i == 0))
    def _():
        padded = jnp.floor((cnt[...] + (MOE_BLOCK - 1)) * (1.0 / MOE_BLOCK)) * MOE_BLOCK
        r = lax.broadcasted_iota(I32, (N_EXPERTS, N_EXPERTS), 0)
        c = lax.broadcasted_iota(I32, (N_EXPERTS, N_EXPERTS), 1)
        pend = jnp.dot((c <= r).astype(F32), padded, precision=HI, preferred_element_type=F32)
        pstart[...] = pend - padded
        carry[...] = jnp.zeros_like(carry)
        cnt_ref[...] = cnt[...]

    @pl.when(ph == 1)
    def _():
        rr = lax.broadcasted_iota(I32, (tl, tl), 0)
        cc = lax.broadcasted_iota(I32, (tl, tl), 1)
        before = _dot(ind.astype(BF16), (rr < cc).astype(BF16))
        base = before + carry[:, 0:1] + pstart[:, 0:1]
        den = jnp.exp(vals[0] - vals[0])
        for k in range(1, TOP_K):
            den = den + jnp.exp(vals[k] - vals[0])
        g_ref[...] = jnp.zeros_like(g_ref)
        for k in range(TOP_K):
            g_ref[k:k + 1, :] = jnp.exp(vals[k] - vals[0]) / den
            dest = jnp.sum(jnp.where(hots[k], base, 0.0), axis=0, keepdims=True).astype(I32)
            for t in range(ROUTE_STEP // ROUTE_TILE):
                d_ref[t, k:k + 1, :] = dest[:, ROUTE_TILE * t:ROUTE_TILE * t + ROUTE_TILE]
        carry[...] = carry[...] + tile_cnt


def _two_source_maps(n_first, trailing=1):
    pad = (0,) * trailing
    return (lambda *idx: (jnp.minimum(idx[-1], n_first - 1),) + pad,
            lambda *idx: (jnp.maximum(idx[-1] - n_first, 0),) + pad)


def _route(lg_a, lg_b):
    ta = lg_a.shape[1] + lg_b.shape[1]
    ns = ta // ROUTE_STEP
    n_first = lg_a.shape[1] // ROUTE_STEP
    per_step = ROUTE_STEP // ROUTE_TILE
    assert n_first * ROUTE_STEP == lg_a.shape[1] and ns * ROUTE_STEP == ta
    return pl.pallas_call(
        functools.partial(_route_kernel, n_first),
        grid=(2, ns),
        in_specs=[pl.BlockSpec((N_EXPERTS, ROUTE_STEP), lambda p, i: (0, jnp.minimum(i, n_first - 1))),
                  pl.BlockSpec((N_EXPERTS, ROUTE_STEP), lambda p, i: (0, jnp.maximum(i - n_first, 0)))],
        out_specs=(pl.BlockSpec((8, ROUTE_STEP), lambda p, i: (0, i * p)),
                   pl.BlockSpec((per_step, TOP_K, ROUTE_TILE), lambda p, i: (i * p, 0, 0)),
                   pl.BlockSpec((N_EXPERTS, LANE), lambda p, i: (0, 0))),
        out_shape=(jax.ShapeDtypeStruct((8, ta), F32),
                   jax.ShapeDtypeStruct((ta // ROUTE_TILE, TOP_K, ROUTE_TILE), I32),
                   jax.ShapeDtypeStruct((N_EXPERTS, LANE), F32)),
        scratch_shapes=[pltpu.VMEM((N_EXPERTS, LANE), F32)] * 3,
        compiler_params=pltpu.CompilerParams(dimension_semantics=("arbitrary", "arbitrary")),
        name="route",
    )(lg_a, lg_b)


DMA_UNROLL = 8


def _row_copy(src, dst, sem):
    return pltpu.make_async_copy(src, dst, sem)


def _dispatch_kernel(n_first, pend_ref, padded_ref, nu_ref, d_ref, xa_ref, xb_ref, o_ref, zeros, sem):
    i = pl.program_id(0)
    tl = ROUTE_TILE
    nb = o_ref.shape[0] // MOE_BLOCK

    @pl.when(i == 0)
    def _():
        zeros[...] = jnp.zeros_like(zeros)

        def fills(e):
            return ((padded_ref[e] > 0, pend_ref[e] - MOE_BLOCK),
                    (nu_ref[0] + e < nb, (nu_ref[0] + e) * MOE_BLOCK))

        for e in range(N_EXPERTS):
            for cond, row in fills(e):
                @pl.when(cond)
                def _(row=row):
                    _row_copy(zeros, o_ref.at[pl.ds(row, MOE_BLOCK)], sem).start()
        for e in range(N_EXPERTS):
            for cond, row in fills(e):
                @pl.when(cond)
                def _(row=row):
                    _row_copy(zeros, o_ref.at[pl.ds(row, MOE_BLOCK)], sem).wait()

    def scatter_rows(x_ref):
        def start(j2, c):
            for u in range(DMA_UNROLL):
                j = j2 * DMA_UNROLL + u
                for k in range(TOP_K):
                    _row_copy(x_ref.at[j], o_ref.at[d_ref[k, j]], sem).start(priority=(u + k) % 2)
            return c

        lax.fori_loop(0, tl // DMA_UNROLL, start, 0)

        def wait(j2, c):
            for u in range(DMA_UNROLL):
                j = j2 * DMA_UNROLL + u
                for k in range(TOP_K):
                    _row_copy(x_ref.at[j], o_ref.at[d_ref[k, j]], sem).wait()
            return c

        lax.fori_loop(0, tl // DMA_UNROLL, wait, 0)

    @pl.when(i < n_first)
    def _():
        scatter_rows(xa_ref)

    @pl.when(i >= n_first)
    def _():
        scatter_rows(xb_ref)


def _dispatch(xa, xb, d3, pend, padded, n_used, n_rows):
    nt = (xa.shape[0] + xb.shape[0]) // ROUTE_TILE
    n_first = xa.shape[0] // ROUTE_TILE
    map_a, map_b = _two_source_maps(n_first, trailing=2)
    grid_spec = pltpu.PrefetchScalarGridSpec(
        num_scalar_prefetch=3,
        grid=(nt,),
        in_specs=[pl.BlockSpec((None, TOP_K, ROUTE_TILE), lambda i, *_: (i, 0, 0), memory_space=pltpu.SMEM),
                  pl.BlockSpec((ROUTE_TILE, ROW_TILES, LANE), lambda i, *_: map_a(i)),
                  pl.BlockSpec((ROUTE_TILE, ROW_TILES, LANE), lambda i, *_: map_b(i))],
        out_specs=pl.BlockSpec(memory_space=pl.ANY),
        scratch_shapes=[pltpu.VMEM((MOE_BLOCK, ROW_TILES, LANE), F32), pltpu.SemaphoreType.DMA],
    )
    return pl.pallas_call(
        functools.partial(_dispatch_kernel, n_first),
        grid_spec=grid_spec,
        out_shape=jax.ShapeDtypeStruct((n_rows, ROW_TILES, LANE), F32),
        compiler_params=pltpu.CompilerParams(dimension_semantics=("arbitrary",)),
        name="dispatch",
    )(pend, padded, n_used, d3, xa, xb)


def _expert_kernel(be_ref, nu_ref, x_ref, wg_ref, bg_ref, wu_ref, bu_ref, wd_ref, bd_ref, o_ref, wg_s, wu_s, wd_s,
                   lhs_s):
    i = pl.program_id(0)
    prev = be_ref[jnp.maximum(i - 1, 0)]
    fresh = jnp.logical_or(i == 0, be_ref[i] != prev)

    @pl.when(jnp.logical_and(i < nu_ref[0], fresh))
    def _():
        wg_s[...] = wg_ref[0].astype(BF16)
        wu_s[...] = wu_ref[0].astype(BF16)
        wd_s[...] = wd_ref[0].astype(BF16)

    @pl.when(i < nu_ref[0])
    def _():
        for s, slab in enumerate(_load_rows(x_ref)):
            lhs_s[:, LANE * s:LANE * s + LANE] = slab.astype(BF16)
        x = lhs_s[...]
        g = _dot(x, wg_s[...]) + bg_ref[0]
        u = _dot(x, wu_s[...]) + bu_ref[0]
        g = jnp.minimum(g, SWIGLU_LIMIT)
        u = jnp.clip(u, -SWIGLU_LIMIT, SWIGLU_LIMIT)
        act = g * jax.nn.sigmoid(SWIGLU_ALPHA * g) * (u + 1.0)
        _store_rows(o_ref, _dot(act.astype(BF16), wd_s[...]) + bd_ref[0])

    @pl.when(i >= nu_ref[0])
    def _():
        o_ref[...] = jnp.zeros_like(o_ref)


def _experts(xs, block_e, n_used, wg, bg, wu, bu, wd, bd):
    n_rows = xs.shape[0]
    nb = n_rows // MOE_BLOCK
    row_map = lambda i, be, nu: (jnp.maximum(jnp.minimum(i, nu[0] - 1), 0), 0, 0)
    w_map = lambda i, be, nu: (be[i], 0, 0)
    w_spec = pl.BlockSpec((1, D_MODEL, D_MODEL), w_map)
    b_spec = pl.BlockSpec((1, 1, D_MODEL), w_map)
    grid_spec = pltpu.PrefetchScalarGridSpec(
        num_scalar_prefetch=2,
        grid=(nb,),
        in_specs=[pl.BlockSpec((MOE_BLOCK, ROW_TILES, LANE), row_map), w_spec, b_spec, w_spec, b_spec, w_spec,
                  b_spec],
        out_specs=pl.BlockSpec((MOE_BLOCK, ROW_TILES, LANE), lambda i, be, nu: (i, 0, 0)),
        scratch_shapes=[pltpu.VMEM((D_MODEL, D_MODEL), BF16)] * 3 + [pltpu.VMEM((MOE_BLOCK, D_MODEL), BF16)],
    )
    return pl.pallas_call(
        _expert_kernel,
        grid_spec=grid_spec,
        out_shape=jax.ShapeDtypeStruct((n_rows, ROW_TILES, LANE), F32),
        compiler_params=pltpu.CompilerParams(dimension_semantics=("arbitrary",), vmem_limit_bytes=VMEM_LIMIT),
        name="experts",
    )(block_e, n_used, xs, wg, bg.reshape(N_EXPERTS, 1, D_MODEL), wu, bu.reshape(N_EXPERTS, 1, D_MODEL), wd,
      bd.reshape(N_EXPERTS, 1, D_MODEL))


def _combine_kernel(n_first, d_ref, x1a_ref, x1b_ref, g_ref, nf_ref, eo_ref, ya_ref, yb_ref, buf, y_s, sem):
    i = pl.program_id(0)
    tl = ROUTE_TILE

    def gather(start):
        def body(j2, c):
            for u in range(DMA_UNROLL):
                j = j2 * DMA_UNROLL + u
                for k in range(TOP_K):
                    cp = _row_copy(eo_ref.at[d_ref[k, j]], buf.at[k, j], sem)
                    if start:
                        cp.start(priority=(u + k) % 2)
                    else:
                        cp.wait()
            return c

        lax.fori_loop(0, tl // DMA_UNROLL, body, 0)

    gather(True)
    sel = (lax.broadcasted_iota(I32, (8, LANE), 0) == lax.broadcasted_iota(I32, (8, LANE), 1)).astype(F32)
    gt = _dot_tn(g_ref[...], sel, precision=HI)
    gather(False)
    for k in range(TOP_K):
        for s, slab in enumerate(_load_rows(buf.at[k])):
            cols = slice(LANE * s, LANE * s + LANE)
            term = gt[:, k:k + 1] * slab
            y_s[:, cols] = term if k == 0 else y_s[:, cols] + term
    y = jnp.where(i < n_first, x1a_ref[...], x1b_ref[...]) + y_s[...]
    y = y * lax.rsqrt(jnp.mean(y * y, axis=-1, keepdims=True) + EPS) * nf_ref[...]

    @pl.when(i < n_first)
    def _():
        ya_ref[...] = y

    @pl.when(i >= n_first)
    def _():
        yb_ref[...] = y


def _combine(x1a, x1b, gates8, d3, eo3, nf):
    nt = (x1a.shape[0] + x1b.shape[0]) // ROUTE_TILE
    n_first = x1a.shape[0] // ROUTE_TILE
    map_a, map_b = _two_source_maps(n_first)
    row_specs = (pl.BlockSpec((ROUTE_TILE, D_MODEL), map_a), pl.BlockSpec((ROUTE_TILE, D_MODEL), map_b))
    grid_spec = pltpu.PrefetchScalarGridSpec(
        num_scalar_prefetch=0,
        grid=(nt,),
        in_specs=[pl.BlockSpec((None, TOP_K, ROUTE_TILE), lambda i: (i, 0, 0), memory_space=pltpu.SMEM),
                  *row_specs,
                  pl.BlockSpec((8, ROUTE_TILE), lambda i: (0, i)),
                  _const_spec(nf.shape),
                  pl.BlockSpec(memory_space=pl.ANY)],
        out_specs=row_specs,
        scratch_shapes=[pltpu.VMEM((TOP_K, ROUTE_TILE, ROW_TILES, LANE), F32),
                        pltpu.VMEM((ROUTE_TILE, D_MODEL), F32), pltpu.SemaphoreType.DMA],
    )
    return pl.pallas_call(
        functools.partial(_combine_kernel, n_first),
        grid_spec=grid_spec,
        out_shape=(jax.ShapeDtypeStruct(x1a.shape, F32), jax.ShapeDtypeStruct(x1b.shape, F32)),
        compiler_params=pltpu.CompilerParams(dimension_semantics=("arbitrary",), vmem_limit_bytes=VMEM_LIMIT),
        name="combine",
    )(d3, x1a, x1b, gates8, nf, eo3)


def _moe_and_norm(x1, xn2, lgt, wg, bg, wu, bu, wd, bd, nf):
    ta = x1[0].shape[0] + x1[1].shape[0]
    n_rows = (-(-(ta * TOP_K) // MOE_BLOCK) + N_EXPERTS) * MOE_BLOCK
    gates8, d3, cnt = _route(*lgt)
    counts = cnt[:, 0].astype(I32)
    padded = (counts + MOE_BLOCK - 1) // MOE_BLOCK * MOE_BLOCK
    pend = jnp.cumsum(padded)
    n_used = (pend[-1:] // MOE_BLOCK).astype(I32)
    block_row = jnp.arange(n_rows // MOE_BLOCK, dtype=I32) * MOE_BLOCK
    block_e = jnp.minimum(jnp.sum((pend[None, :] <= block_row[:, None]).astype(I32), axis=1), N_EXPERTS - 1)
    xs = _dispatch(*xn2, d3, pend.astype(I32), padded.astype(I32), n_used, n_rows)
    eo = _experts(xs, block_e, n_used, wg, bg, wu, bu, wd, bd)
    return _combine(*x1, gates8, d3, eo, nf)


def _prep_consts(norm1_w, w_in, conv_w, conv_b, dt_bias, A_log, D_skip, ssd_norm_w, hgrn_lower_bound, hgrn_norm_w,
                 w_out, norm2_w, w_router, b_router):
    row = lambda v: v.reshape(1, -1).astype(F32)
    pad_lane = lambda v: jnp.pad(v.reshape(1, -1).astype(F32), ((0, 0), (0, LANE - v.size)))
    wa = w_in[:, :Q0].astype(BF16)
    wq = w_in[:, Q0 + SSD_HEADS:].astype(BF16)
    wdt = jnp.pad(w_in[:, Q0:Q0 + SSD_HEADS], ((0, 0), (0, LANE - SSD_HEADS))).astype(BF16)
    dskip = jnp.repeat(D_skip.astype(F32), SSD_P).reshape(1, -1)
    return (row(norm1_w), wa, wq, wdt, conv_w.astype(F32), row(conv_b), pad_lane(dt_bias), pad_lane(A_log), dskip,
            row(ssd_norm_w), hgrn_lower_bound.astype(F32), row(hgrn_norm_w), w_out.astype(BF16), row(norm2_w),
            w_router.T.astype(F32), b_router.reshape(-1, 1).astype(F32))


def kernel(x_prompt, x_sample, state_conv, state_ssm, state_hgrn, norm1_w, w_in, conv_w, conv_b, dt_bias, A_log, D_skip, ssd_norm_w, hgrn_lower_bound, hgrn_norm_w, w_out, norm2_w, w_router, b_router, w_gate, b_gate, w_up, b_up, w_down, b_down, norm_f_w):
    depth = w_in.shape[0]
    assert depth == 1
    bsz, seq, _ = x_prompt.shape
    nseq, tdec, _ = x_sample.shape
    consts = _prep_consts(norm1_w[0], w_in[0], conv_w[0], conv_b[0], dt_bias[0], A_log[0], D_skip[0], ssd_norm_w[0],
                          hgrn_lower_bound, hgrn_norm_w[0], w_out[0], norm2_w[0], w_router[0], b_router[0])
    n1, wa, wq, wdt, cw, cb, dtb, alog, dskip, snw, hlb, hnw, wout, n2, wrt, br = consts
    chunk_consts = (cw, cb, dtb, alog, dskip, snw, hlb, hnw)

    x1_p, xn2_p, lg_p, conv_p, ssm_p, hgrn_p = _prompt_mixer(x_prompt, consts)

    xs_flat = x_sample.reshape(nseq * tdec, D_MODEL)
    proj_s = _sample_inproj(xs_flat, n1, (wa, wq, wdt))
    mix_s, conv_s, ssm_s, hgrn_s = _sample_step(
        proj_s, state_conv[0], state_ssm[0].reshape(nseq, SSD_HEADS * SSD_P, SSD_N),
        state_hgrn[0].reshape(nseq, HGRN_HEADS * HGRN_D, HGRN_D), chunk_consts)
    x1_s, xn2_s, lg_s = _sample_out(mix_s, xs_flat, wout, n2, wrt, br)

    y_p, y_s = _moe_and_norm((x1_p, x1_s), (xn2_p, xn2_s), (lg_p, lg_s), w_gate[0], b_gate[0], w_up[0], b_up[0],
                             w_down[0], b_down[0], norm_f_w.reshape(1, -1).astype(F32))
    y_prompt = y_p.reshape(bsz, seq, D_MODEL)
    y_sample = y_s.reshape(nseq, tdec, D_MODEL)
    return (y_prompt, y_sample,
            conv_p[None], ssm_p.reshape(1, bsz, SSD_HEADS, SSD_P, SSD_N),
            hgrn_p.reshape(1, bsz, HGRN_HEADS, HGRN_D, HGRN_D),
            conv_s[None], ssm_s.reshape(1, nseq, SSD_HEADS, SSD_P, SSD_N),
            hgrn_s.reshape(1, nseq, HGRN_HEADS, HGRN_D, HGRN_D))
```

```python
import functools

import jax
import jax.numpy as jnp
from jax import lax
from jax.experimental import pallas as pl
from jax.experimental.pallas import tpu as pltpu

F32 = jnp.float32
BF16 = jnp.bfloat16
I32 = jnp.int32
HI = lax.Precision.HIGHEST

D_MODEL = 1024
SSD_HEADS = 16
SSD_P = 64
SSD_N = 128
CONV_DIM = 1536
CONV_W = 4
HGRN_HEADS = 8
HGRN_D = 128
N_EXPERTS = 32
TOP_K = 4
MOE_BLOCK = 512
SWIGLU_LIMIT = 7.0
SWIGLU_ALPHA = 1.702
EPS = 1e-6

Z0 = 0
X0 = 1024
Q0 = 2560
F0 = 3584
I0 = 4608
G0 = 5632
DT0 = 6656
PW = 6784
LANE = 128
HALO = 8
SSD_CHUNK = 128
HGRN_CHUNK = 16
MAX_SUB = SSD_CHUNK // HGRN_CHUNK
VMEM_LIMIT = 56 * 1024 * 1024


def _silu(x):
    return x * jax.nn.sigmoid(x)


def _softplus(x):
    return jnp.maximum(x, 0.0) + jnp.log1p(jnp.exp(-jnp.abs(x)))


def _dot(a, b):
    return jnp.dot(a, b, preferred_element_type=F32)


def _dot_nt(a, b, precision=None):
    return lax.dot_general(a, b, (((1,), (1,)), ((), ())), precision=precision, preferred_element_type=F32)


def _dot_tn(a, b, precision=None):
    return lax.dot_general(a, b, (((0,), (0,)), ((), ())), precision=precision, preferred_element_type=F32)


def _split3(x):
    hi = x.astype(BF16)
    r1 = x - hi.astype(F32)
    mid = r1.astype(BF16)
    lo = (r1 - mid.astype(F32)).astype(BF16)
    return hi, mid, lo


def _sel_dot(sel, x):
    return sum(_dot(sel, p) for p in _split3(x))


def _sel_dot_tn(x, sel):
    return sum(_dot_tn(p, sel) for p in _split3(x))


WIN_PARTS = (Z0, Q0, DT0)


def _in_proj(x, n1_ref, win_refs, pj, r0):
    rows = x.shape[0]
    ms = jnp.mean(x * x, axis=-1, keepdims=True)
    h = (x * lax.rsqrt(ms + EPS) * n1_ref[...]).astype(BF16)
    for w_ref, base in zip(win_refs, WIN_PARTS):
        width = w_ref.shape[1]
        for c0 in range(0, width, 512):
            c1 = min(c0 + 512, width)
            pj[r0:r0 + rows, base + c0:base + c1] = _dot(h, w_ref[:, c0:c1])


def _mix_chunk(pj, r0, L, nv, LH, ssm, hst, prm, scr, mix_s, m0):
    cw, cb, dtb, alog, dskip, snw, hlb, hnw = prm
    y_s, b_s, kk_s, r_s, q_s, qb_s, v_s, kend_s = scr

    def conv_cols(c0, c1):
        acc = cb[:, c0:c1]
        for k in range(CONV_W):
            acc = acc + pj[r0 - 3 + k:r0 - 3 + k + L, X0 + c0:X0 + c1] * cw[k:k + 1, c0:c1]
        return _silu(acc)

    rowi = lax.broadcasted_iota(I32, (L, L), 0)
    coli = lax.broadcasted_iota(I32, (L, L), 1)
    causal = coli <= rowi
    rid = lax.broadcasted_iota(I32, (L, LANE), 0)
    lane = lax.broadcasted_iota(I32, (L, LANE), 1)
    lo = lane < SSD_P

    dt = _softplus(pj[r0:r0 + L, DT0:DT0 + LANE] + dtb[...])
    if nv < L:
        dt = jnp.where(rid < nv, dt, 0.0)
    a = dt * (-jnp.exp(alog[...]))
    acum = _sel_dot(causal.astype(BF16), a)
    acum_t = _sel_dot_tn(a, (rowi <= coli).astype(BF16))
    dt_t = _sel_dot_tn(dt, (rowi == coli).astype(BF16))
    eac = jnp.exp(acum)
    last = acum[L - 1:L, :]
    elast = jnp.exp(last)
    tail = jnp.exp(last - acum) * dt

    bm = [conv_cols(1024 + 128 * g, 1152 + 128 * g).astype(BF16) for g in range(2)]
    cm = [conv_cols(1280 + 128 * g, 1408 + 128 * g).astype(BF16) for g in range(2)]
    gcb = [_dot_nt(cm[g], bm[g]) for g in range(2)]
    rowh = lax.broadcasted_iota(I32, (LANE, LANE), 0) < SSD_P

    for j in range(SSD_HEADS // 2):
        g = j // 4
        cols = slice(128 * j, 128 * j + 128)
        xs = conv_cols(128 * j, 128 * j + 128)
        xsb = xs.astype(BF16)
        ys = []
        for h in (2 * j, 2 * j + 1):
            seg = acum[:, h:h + 1] - acum_t[h:h + 1, :]
            dec = jnp.exp(jnp.where(causal, seg, -jnp.inf))
            mh = (gcb[g] * dec * dt_t[h:h + 1, :]).astype(BF16)
            ys.append(_dot(mh, xsb))
        y = jnp.where(lo, ys[0], ys[1])
        hp = ssm[128 * j:128 * j + 128, :]
        yoff = _dot_nt(cm[g], hp.astype(BF16))
        y = y + yoff * jnp.where(lo, eac[:, 2 * j:2 * j + 1], eac[:, 2 * j + 1:2 * j + 2])
        y = y + dskip[:, cols] * xs
        y_s[0:L, cols] = y
        xt = (xs * jnp.where(lo, tail[:, 2 * j:2 * j + 1], tail[:, 2 * j + 1:2 * j + 2])).astype(BF16)
        upd = _dot_tn(xt, bm[g])
        el = jnp.where(rowh, elast[:, 2 * j:2 * j + 1], elast[:, 2 * j + 1:2 * j + 2])
        ssm[128 * j:128 * j + 128, :] = el * hp + upd

    yz = y_s[0:L, :] * _silu(pj[r0:r0 + L, Z0:Z0 + 1024])
    for g in range(2):
        sl = slice(512 * g, 512 * g + 512)
        part = yz[:, sl]
        ms = jnp.mean(part * part, axis=-1, keepdims=True)
        mix_s[m0:m0 + L, sl] = (part * lax.rsqrt(ms + EPS) * snw[:, sl]).astype(mix_s.dtype)

    n_sub = L // LH
    AH = min(L, 2 * LH)
    n_att = L // AH
    r0v = hlb[0:1, :]
    r1v = hlb[1:2, :]
    mx = jnp.maximum(r0v, r1v)
    e0 = jnp.exp(r0v - mx)
    lb = e0 / (e0 + jnp.exp(r1v - mx))
    rid_w = lax.broadcasted_iota(I32, (L, 1024), 0)
    f = lb + (1.0 - lb) * jax.nn.sigmoid(pj[r0:r0 + L, F0:F0 + 1024])
    if nv < L:
        f = jnp.where(rid_w < nv, f, 1.0)
    kk = 1.0 - f
    blk = jnp.logical_and(causal, (rowi // LH) == (coli // LH)).astype(F32)
    b = _sel_dot(blk.astype(BF16), jnp.log(f))
    q = _silu(pj[r0:r0 + L, Q0:Q0 + 1024])
    base = jnp.zeros((1, 1024), F32)
    for i in range(n_sub):
        rows = slice(LH * i, LH * i + LH)
        bi = b[rows, :] + base
        b_s[rows, :] = bi
        qb_s[rows, :] = (q[rows, :] * jnp.exp(bi)).astype(BF16)
        base = bi[LH - 1:LH, :]
    for i in range(n_att):
        rows = slice(AH * i, AH * i + AH)
        mid = b_s[AH * i + AH // 2 - 1:AH * i + AH // 2, :]
        r_s[i:i + 1, :] = mid
        q_s[rows, :] = (q[rows, :] * jnp.exp(b_s[rows, :] - mid)).astype(BF16)
    kk_s[0:L, :] = kk
    v_s[0:L, :] = pj[r0:r0 + L, I0:I0 + 1024].astype(BF16)
    kend_s[0:L, :] = (kk * jnp.exp(base - b_s[0:L, :])).astype(BF16)
    ebl = jnp.exp(base)
    colq = lax.broadcasted_iota(I32, (AH, L), 1)
    rowq = lax.broadcasted_iota(I32, (AH, L), 0)

    for hh in range(HGRN_HEADS):
        c = slice(128 * hh, 128 * hh + 128)
        bh = b_s[0:L, c]
        kkh = kk_s[0:L, c]
        parts = []
        for i in range(n_att):
            n_keys = AH * i + AH
            ktil = (kkh[0:n_keys, :] * jnp.exp(r_s[i:i + 1, c] - bh[0:n_keys, :])).astype(BF16)
            if n_keys < L:
                ktil = jnp.concatenate([ktil, jnp.zeros((L - n_keys, LANE), BF16)], axis=0)
            att = _dot_nt(q_s[AH * i:AH * i + AH, c], ktil)
            parts.append(jnp.where(colq <= rowq + AH * i, att, 0.0).astype(BF16))
        att_all = parts[0] if n_att == 1 else jnp.concatenate(parts, axis=0)
        st = hst[c, :]
        oh = _dot(att_all, v_s[0:L, c]) + _dot_nt(qb_s[0:L, c], st.astype(BF16))
        hst[c, :] = ebl[:, c] * st + _dot_tn(v_s[0:L, c], kend_s[0:L, c])
        ms = jnp.mean(oh * oh, axis=-1, keepdims=True)
        on = oh * lax.rsqrt(ms + EPS) * hnw[:, c]
        mix_s[m0:m0 + L, 1024 + 128 * hh:1152 + 128 * hh] = (
            on * _silu(pj[r0:r0 + L, G0 + 128 * hh:G0 + 128 * hh + 128])).astype(mix_s.dtype)


def _chunk_scratch(rows):
    wide = lambda dt: pltpu.VMEM((rows, 1024), dt)
    return [wide(F32), wide(F32), wide(F32), pltpu.VMEM((MAX_SUB, 1024), F32), wide(BF16), wide(BF16), wide(BF16),
            wide(BF16)]


ROW_TILES = D_MODEL // LANE


def _store_rows(ref, val):
    for s in range(ROW_TILES):
        ref[:, s, :] = val[:, LANE * s:LANE * s + LANE]


def _load_rows(ref):
    planes = jnp.swapaxes(ref[...], 0, 1)
    return [planes[s] for s in range(ROW_TILES)]


def _out_part(mix, x, wout_ref, n2_ref, wrt_ref, br_ref):
    x1 = x + _dot(mix, wout_ref[...])
    ms = jnp.mean(x1 * x1, axis=-1, keepdims=True)
    xn2 = x1 * lax.rsqrt(ms + EPS) * n2_ref[...]
    lgt = _dot_nt(wrt_ref[...], xn2, precision=HI) + br_ref[...]
    return x1, xn2, lgt


def _prompt_kernel(tt, nt, x_ref, n1_ref, wa_ref, wb_ref, wdt_ref, cw, cb, dtb, alog, dskip, snw, hlb, hnw, wout_ref,
                   n2_ref, wrt_ref, br_ref, x1_ref, xn2_ref, lg_ref, conv_ref, ssm_ref, hgrn_ref,
                   pj, mix_s, hst, *scr):
    t = pl.program_id(1)

    @pl.when(t == 0)
    def _():
        pj[0:HALO, :] = jnp.zeros((HALO, PW), F32)
        ssm_ref[0] = jnp.zeros((SSD_HEADS * SSD_P, SSD_N), F32)
        hst[...] = jnp.zeros_like(hst)

    x = x_ref[0]
    _in_proj(x, n1_ref, (wa_ref, wb_ref, wdt_ref), pj, HALO)
    prm = (cw, cb, dtb, alog, dskip, snw, hlb, hnw)
    n_scr = len(scr) // (tt // SSD_CHUNK)
    for c in range(tt // SSD_CHUNK):
        _mix_chunk(pj, HALO + SSD_CHUNK * c, SSD_CHUNK, SSD_CHUNK, HGRN_CHUNK, ssm_ref.at[0], hst, prm,
                   scr[n_scr * c:n_scr * c + n_scr], mix_s, SSD_CHUNK * c)
    x1, xn2, lgt = _out_part(mix_s[...], x, wout_ref, n2_ref, wrt_ref, br_ref)
    x1_ref[...] = x1
    _store_rows(xn2_ref, xn2)
    lg_ref[...] = lgt

    @pl.when(t == nt - 1)
    def _():
        conv_ref[0] = pj[HALO + tt - 3:HALO + tt, X0:X0 + CONV_DIM]
        for hh in range(HGRN_HEADS):
            c = slice(128 * hh, 128 * hh + 128)
            hgrn_ref[0, c, :] = hst[c, :].T

    pj[0:HALO, X0:X0 + CONV_DIM] = pj[tt:tt + HALO, X0:X0 + CONV_DIM]


def _const_spec(shape):
    nd = len(shape)
    return pl.BlockSpec(shape, lambda *_: (0,) * nd, pipeline_mode=pl.Buffered(1))


def _prompt_mixer(x, consts, tt=256):
    bsz, seq, _ = x.shape
    nt = seq // tt
    ta = bsz * seq
    const_specs = [_const_spec(c.shape) for c in consts]
    out_shape = (
        jax.ShapeDtypeStruct((ta, D_MODEL), F32),
        jax.ShapeDtypeStruct((ta, ROW_TILES, LANE), F32),
        jax.ShapeDtypeStruct((N_EXPERTS, ta), F32),
        jax.ShapeDtypeStruct((bsz, CONV_W - 1, CONV_DIM), F32),
        jax.ShapeDtypeStruct((bsz, SSD_HEADS * SSD_P, SSD_N), F32),
        jax.ShapeDtypeStruct((bsz, HGRN_HEADS * HGRN_D, HGRN_D), F32),
    )
    out_specs = (
        pl.BlockSpec((tt, D_MODEL), lambda b, t: (b * nt + t, 0)),
        pl.BlockSpec((tt, ROW_TILES, LANE), lambda b, t: (b * nt + t, 0, 0)),
        pl.BlockSpec((N_EXPERTS, tt), lambda b, t: (0, b * nt + t)),
        pl.BlockSpec((1, CONV_W - 1, CONV_DIM), lambda b, t: (b, 0, 0)),
        pl.BlockSpec((1, SSD_HEADS * SSD_P, SSD_N), lambda b, t: (b, 0, 0)),
        pl.BlockSpec((1, HGRN_HEADS * HGRN_D, HGRN_D), lambda b, t: (b, 0, 0)),
    )
    scratch = [
        pltpu.VMEM((HALO + tt, PW), F32),
        pltpu.VMEM((tt, 2048), BF16),
        pltpu.VMEM((HGRN_HEADS * HGRN_D, HGRN_D), F32),
    ] + _chunk_scratch(SSD_CHUNK) * (tt // SSD_CHUNK)
    return pl.pallas_call(
        functools.partial(_prompt_kernel, tt, nt),
        grid=(bsz, nt),
        in_specs=[pl.BlockSpec((1, tt, D_MODEL), lambda b, t: (b, t, 0))] + const_specs,
        out_specs=out_specs,
        out_shape=out_shape,
        scratch_shapes=scratch,
        compiler_params=pltpu.CompilerParams(dimension_semantics=("arbitrary", "arbitrary"),
                                             vmem_limit_bytes=VMEM_LIMIT),
        name="prompt_mixer",
    )(x, *consts)


def _sample_inproj_kernel(x_ref, n1_ref, wa_ref, wb_ref, wdt_ref, o_ref):
    _in_proj(x_ref[...], n1_ref, (wa_ref, wb_ref, wdt_ref), o_ref, 0)


def _sample_inproj(x, n1, win, rows=256):
    n = x.shape[0]
    return pl.pallas_call(
        _sample_inproj_kernel,
        grid=(n // rows,),
        in_specs=[pl.BlockSpec((rows, D_MODEL), lambda i: (i, 0)), _const_spec(n1.shape)]
        + [_const_spec(w.shape) for w in win],
        out_specs=pl.BlockSpec((rows, PW), lambda i: (i, 0)),
        out_shape=jax.ShapeDtypeStruct((n, PW), F32),
        compiler_params=pltpu.CompilerParams(dimension_semantics=("arbitrary",), vmem_limit_bytes=VMEM_LIMIT),
        name="sample_inproj",
    )(x, n1, *win)


SAMPLE_L = 16


SAMPLE_GROUP = 4


def _sample_step_kernel(tdec, pj_ref, cin_ref, sin_ref, hin_ref, cw, cb, dtb, alog, dskip, snw, hlb, hnw,
                        mix_ref, cout_ref, sout_ref, hout_ref, *scr):
    prm = (cw, cb, dtb, alog, dskip, snw, hlb, hnw)
    n_scr = len(scr) // SAMPLE_GROUP
    for g in range(SAMPLE_GROUP):
        pj, mix_s, hst, *chunk_scr = scr[n_scr * g:n_scr * g + n_scr]
        pj[HALO + tdec:HALO + SAMPLE_L, :] = jnp.zeros((SAMPLE_L - tdec, PW), F32)
        pj[HALO - 3:HALO, X0:X0 + CONV_DIM] = cin_ref[g]
        pj[HALO:HALO + tdec, :] = pj_ref[tdec * g:tdec * g + tdec, :]
        sout_ref[g] = sin_ref[g]
        for hh in range(HGRN_HEADS):
            c = slice(128 * hh, 128 * hh + 128)
            hst[c, :] = hin_ref[g, c, :].T
        _mix_chunk(pj, HALO, SAMPLE_L, tdec, SAMPLE_L, sout_ref.at[g], hst, prm, chunk_scr, mix_s, 0)
        mix_ref[tdec * g:tdec * g + tdec, :] = mix_s[0:tdec, :]
        cout_ref[g] = pj[HALO + tdec - 3:HALO + tdec, X0:X0 + CONV_DIM]
        for hh in range(HGRN_HEADS):
            c = slice(128 * hh, 128 * hh + 128)
            hout_ref[g, c, :] = hst[c, :].T


def _sample_step(proj, conv_in, ssm_in, hgrn_in, consts):
    nseq = conv_in.shape[0]
    tdec = proj.shape[0] // nseq
    assert nseq % SAMPLE_GROUP == 0 and (SAMPLE_GROUP * tdec) % 8 == 0
    seq_spec = lambda shape: pl.BlockSpec((SAMPLE_GROUP,) + shape, lambda i: (i,) + (0,) * len(shape))
    tok_spec = lambda width: pl.BlockSpec((SAMPLE_GROUP * tdec, width), lambda i: (i, 0))
    scratch = ([
        pltpu.VMEM((HALO + SAMPLE_L, PW), F32),
        pltpu.VMEM((SAMPLE_L, 2048), F32),
        pltpu.VMEM((HGRN_HEADS * HGRN_D, HGRN_D), F32),
    ] + _chunk_scratch(SAMPLE_L)) * SAMPLE_GROUP
    return pl.pallas_call(
        functools.partial(_sample_step_kernel, tdec),
        grid=(nseq // SAMPLE_GROUP,),
        in_specs=[tok_spec(PW), seq_spec((CONV_W - 1, CONV_DIM)), seq_spec((SSD_HEADS * SSD_P, SSD_N)),
                  seq_spec((HGRN_HEADS * HGRN_D, HGRN_D))] + [_const_spec(c.shape) for c in consts],
        out_specs=(tok_spec(2048), seq_spec((CONV_W - 1, CONV_DIM)), seq_spec((SSD_HEADS * SSD_P, SSD_N)),
                   seq_spec((HGRN_HEADS * HGRN_D, HGRN_D))),
        out_shape=(jax.ShapeDtypeStruct((nseq * tdec, 2048), F32),
                   jax.ShapeDtypeStruct((nseq, CONV_W - 1, CONV_DIM), F32),
                   jax.ShapeDtypeStruct((nseq, SSD_HEADS * SSD_P, SSD_N), F32),
                   jax.ShapeDtypeStruct((nseq, HGRN_HEADS * HGRN_D, HGRN_D), F32)),
        scratch_shapes=scratch,
        compiler_params=pltpu.CompilerParams(dimension_semantics=("arbitrary",), vmem_limit_bytes=VMEM_LIMIT),
        name="sample_step",
    )(proj, conv_in, ssm_in, hgrn_in, *consts)


def _sample_out_kernel(mix_ref, x_ref, wout_ref, n2_ref, wrt_ref, br_ref, x1_ref, xn2_ref, lg_ref):
    x1, xn2, lgt = _out_part(mix_ref[...].astype(BF16), x_ref[...], wout_ref, n2_ref, wrt_ref, br_ref)
    x1_ref[...] = x1
    _store_rows(xn2_ref, xn2)
    lg_ref[...] = lgt


def _sample_out(mix, x, wout, n2, wrt, br):
    n = x.shape[0]
    return pl.pallas_call(
        _sample_out_kernel,
        out_shape=(jax.ShapeDtypeStruct((n, D_MODEL), F32), jax.ShapeDtypeStruct((n, ROW_TILES, LANE), F32),
                   jax.ShapeDtypeStruct((N_EXPERTS, n), F32)),
        compiler_params=pltpu.CompilerParams(vmem_limit_bytes=VMEM_LIMIT),
        name="sample_out",
    )(mix, x, wout, n2, wrt, br)


ROUTE_TILE = 512
ROUTE_STEP = ROUTE_TILE


def _route_kernel(n_first, lga_ref, lgb_ref, g_ref, d_ref, cnt_ref, cnt, carry, pstart):
    ph = pl.program_id(0)
    i = pl.program_id(1)
    tl = ROUTE_STEP
    l = jnp.where(i < n_first, lga_ref[...], lgb_ref[...])
    eid = lax.broadcasted_iota(I32, (N_EXPERTS, tl), 0)
    hots, vals = [], []
    for _ in range(TOP_K):
        m = jnp.max(l, axis=0, keepdims=True)
        idx = jnp.min(jnp.where(l == m, eid, N_EXPERTS), axis=0, keepdims=True)
        hot = eid == idx
        hots.append(hot)
        vals.append(m)
        l = jnp.where(hot, -jnp.inf, l)
    ind = hots[0].astype(F32)
    for hot in hots[1:]:
        ind = ind + hot.astype(F32)
    tile_cnt = jnp.sum(ind, axis=1, keepdims=True)

    @pl.when(jnp.logical_and(ph == 0, i == 0))
    def _():
        cnt[...] = jnp.zeros_like(cnt)

    @pl.when(ph == 0)
    def _():
        cnt[...] = cnt[...] + tile_cnt

    @pl.when(jnp.logical_and(ph == 1, i == 0))
    def _():
        padded = jnp.floor((cnt[...] + (MOE_BLOCK - 1)) * (1.0 / MOE_BLOCK)) * MOE_BLOCK
        r = lax.broadcasted_iota(I32, (N_EXPERTS, N_EXPERTS), 0)
        c = lax.broadcasted_iota(I32, (N_EXPERTS, N_EXPERTS), 1)
        pend = jnp.dot((c <= r).astype(F32), padded, precision=HI, preferred_element_type=F32)
        pstart[...] = pend - padded
        carry[...] = jnp.zeros_like(carry)
        cnt_ref[...] = cnt[...]

    @pl.when(ph == 1)
    def _():
        rr = lax.broadcasted_iota(I32, (tl, tl), 0)
        cc = lax.broadcasted_iota(I32, (tl, tl), 1)
        before = _dot(ind.astype(BF16), (rr < cc).astype(BF16))
        base = before + carry[:, 0:1] + pstart[:, 0:1]
        den = jnp.exp(vals[0] - vals[0])
        for k in range(1, TOP_K):
            den = den + jnp.exp(vals[k] - vals[0])
        g_ref[...] = jnp.zeros_like(g_ref)
        for k in range(TOP_K):
            g_ref[k:k + 1, :] = jnp.exp(vals[k] - vals[0]) / den
            dest = jnp.sum(jnp.where(hots[k], base, 0.0), axis=0, keepdims=True).astype(I32)
            for t in range(ROUTE_STEP // ROUTE_TILE):
                d_ref[t, k:k + 1, :] = dest[:, ROUTE_TILE * t:ROUTE_TILE * t + ROUTE_TILE]
        carry[...] = carry[...] + tile_cnt


def _two_source_maps(n_first, trailing=1):
    pad = (0,) * trailing
    return (lambda *idx: (jnp.minimum(idx[-1], n_first - 1),) + pad,
            lambda *idx: (jnp.maximum(idx[-1] - n_first, 0),) + pad)


def _route(lg_a, lg_b):
    ta = lg_a.shape[1] + lg_b.shape[1]
    ns = ta // ROUTE_STEP
    n_first = lg_a.shape[1] // ROUTE_STEP
    per_step = ROUTE_STEP // ROUTE_TILE
    assert n_first * ROUTE_STEP == lg_a.shape[1] and ns * ROUTE_STEP == ta
    return pl.pallas_call(
        functools.partial(_route_kernel, n_first),
        grid=(2, ns),
        in_specs=[pl.BlockSpec((N_EXPERTS, ROUTE_STEP), lambda p, i: (0, jnp.minimum(i, n_first - 1))),
                  pl.BlockSpec((N_EXPERTS, ROUTE_STEP), lambda p, i: (0, jnp.maximum(i - n_first, 0)))],
        out_specs=(pl.BlockSpec((8, ROUTE_STEP), lambda p, i: (0, i * p)),
                   pl.BlockSpec((per_step, TOP_K, ROUTE_TILE), lambda p, i: (i * p, 0, 0)),
                   pl.BlockSpec((N_EXPERTS, LANE), lambda p, i: (0, 0))),
        out_shape=(jax.ShapeDtypeStruct((8, ta), F32),
                   jax.ShapeDtypeStruct((ta // ROUTE_TILE, TOP_K, ROUTE_TILE), I32),
                   jax.ShapeDtypeStruct((N_EXPERTS, LANE), F32)),
        scratch_shapes=[pltpu.VMEM((N_EXPERTS, LANE), F32)] * 3,
        compiler_params=pltpu.CompilerParams(dimension_semantics=("arbitrary", "arbitrary")),
        name="route",
    )(lg_a, lg_b)


DMA_UNROLL = 8


def _row_copy(src, dst, sem):
    return pltpu.make_async_copy(src, dst, sem)


def _dispatch_kernel(n_first, pend_ref, padded_ref, nu_ref, d_ref, xa_ref, xb_ref, o_ref, zeros, sem):
    i = pl.program_id(0)
    tl = ROUTE_TILE
    nb = o_ref.shape[0] // MOE_BLOCK

    @pl.when(i == 0)
    def _():
        zeros[...] = jnp.zeros_like(zeros)

        def fills(e):
            return ((padded_ref[e] > 0, pend_ref[e] - MOE_BLOCK),
                    (nu_ref[0] + e < nb, (nu_ref[0] + e) * MOE_BLOCK))

        for e in range(N_EXPERTS):
            for cond, row in fills(e):
                @pl.when(cond)
                def _(row=row):
                    _row_copy(zeros, o_ref.at[pl.ds(row, MOE_BLOCK)], sem).start()
        for e in range(N_EXPERTS):
            for cond, row in fills(e):
                @pl.when(cond)
                def _(row=row):
                    _row_copy(zeros, o_ref.at[pl.ds(row, MOE_BLOCK)], sem).wait()

    def scatter_rows(x_ref):
        def start(j2, c):
            for u in range(DMA_UNROLL):
                j = j2 * DMA_UNROLL + u
                for k in range(TOP_K):
                    _row_copy(x_ref.at[j], o_ref.at[d_ref[k, j]], sem).start(priority=(u + k) % 2)
            return c

        lax.fori_loop(0, tl // DMA_UNROLL, start, 0)

        def wait(j2, c):
            for u in range(DMA_UNROLL):
                j = j2 * DMA_UNROLL + u
                for k in range(TOP_K):
                    _row_copy(x_ref.at[j], o_ref.at[d_ref[k, j]], sem).wait()
            return c

        lax.fori_loop(0, tl // DMA_UNROLL, wait, 0)

    @pl.when(i < n_first)
    def _():
        scatter_rows(xa_ref)

    @pl.when(i >= n_first)
    def _():
        scatter_rows(xb_ref)


def _dispatch(xa, xb, d3, pend, padded, n_used, n_rows):
    nt = (xa.shape[0] + xb.shape[0]) // ROUTE_TILE
    n_first = xa.shape[0] // ROUTE_TILE
    map_a, map_b = _two_source_maps(n_first, trailing=2)
    grid_spec = pltpu.PrefetchScalarGridSpec(
        num_scalar_prefetch=3,
        grid=(nt,),
        in_specs=[pl.BlockSpec((None, TOP_K, ROUTE_TILE), lambda i, *_: (i, 0, 0), memory_space=pltpu.SMEM),
                  pl.BlockSpec((ROUTE_TILE, ROW_TILES, LANE), lambda i, *_: map_a(i)),
                  pl.BlockSpec((ROUTE_TILE, ROW_TILES, LANE), lambda i, *_: map_b(i))],
        out_specs=pl.BlockSpec(memory_space=pl.ANY),
        scratch_shapes=[pltpu.VMEM((MOE_BLOCK, ROW_TILES, LANE), F32), pltpu.SemaphoreType.DMA],
    )
    return pl.pallas_call(
        functools.partial(_dispatch_kernel, n_first),
        grid_spec=grid_spec,
        out_shape=jax.ShapeDtypeStruct((n_rows, ROW_TILES, LANE), F32),
        compiler_params=pltpu.CompilerParams(dimension_semantics=("arbitrary",)),
        name="dispatch",
    )(pend, padded, n_used, d3, xa, xb)


def _expert_kernel(be_ref, nu_ref, x_ref, wg_ref, bg_ref, wu_ref, bu_ref, wd_ref, bd_ref, o_ref, wg_s, wu_s, wd_s,
                   lhs_s):
    i = pl.program_id(0)
    prev = be_ref[jnp.maximum(i - 1, 0)]
    fresh = jnp.logical_or(i == 0, be_ref[i] != prev)

    @pl.when(jnp.logical_and(i < nu_ref[0], fresh))
    def _():
        wg_s[...] = wg_ref[0].astype(BF16)
        wu_s[...] = wu_ref[0].astype(BF16)
        wd_s[...] = wd_ref[0].astype(BF16)

    @pl.when(i < nu_ref[0])
    def _():
        for s, slab in enumerate(_load_rows(x_ref)):
            lhs_s[:, LANE * s:LANE * s + LANE] = slab.astype(BF16)
        x = lhs_s[...]
        g = _dot(x, wg_s[...]) + bg_ref[0]
        u = _dot(x, wu_s[...]) + bu_ref[0]
        g = jnp.minimum(g, SWIGLU_LIMIT)
        u = jnp.clip(u, -SWIGLU_LIMIT, SWIGLU_LIMIT)
        act = g * jax.nn.sigmoid(SWIGLU_ALPHA * g) * (u + 1.0)
        _store_rows(o_ref, _dot(act.astype(BF16), wd_s[...]) + bd_ref[0])

    @pl.when(i >= nu_ref[0])
    def _():
        o_ref[...] = jnp.zeros_like(o_ref)


def _experts(xs, block_e, n_used, wg, bg, wu, bu, wd, bd):
    n_rows = xs.shape[0]
    nb = n_rows // MOE_BLOCK
    row_map = lambda i, be, nu: (jnp.maximum(jnp.minimum(i, nu[0] - 1), 0), 0, 0)
    w_map = lambda i, be, nu: (be[i], 0, 0)
    w_spec = pl.BlockSpec((1, D_MODEL, D_MODEL), w_map)
    b_spec = pl.BlockSpec((1, 1, D_MODEL), w_map)
    grid_spec = pltpu.PrefetchScalarGridSpec(
        num_scalar_prefetch=2,
        grid=(nb,),
        in_specs=[pl.BlockSpec((MOE_BLOCK, ROW_TILES, LANE), row_map), w_spec, b_spec, w_spec, b_spec, w_spec,
                  b_spec],
        out_specs=pl.BlockSpec((MOE_BLOCK, ROW_TILES, LANE), lambda i, be, nu: (i, 0, 0)),
        scratch_shapes=[pltpu.VMEM((D_MODEL, D_MODEL), BF16)] * 3 + [pltpu.VMEM((MOE_BLOCK, D_MODEL), BF16)],
    )
    return pl.pallas_call(
        _expert_kernel,
        grid_spec=grid_spec,
        out_shape=jax.ShapeDtypeStruct((n_rows, ROW_TILES, LANE), F32),
        compiler_params=pltpu.CompilerParams(dimension_semantics=("arbitrary",), vmem_limit_bytes=VMEM_LIMIT),
        name="experts",
    )(block_e, n_used, xs, wg, bg.reshape(N_EXPERTS, 1, D_MODEL), wu, bu.reshape(N_EXPERTS, 1, D_MODEL), wd,
      bd.reshape(N_EXPERTS, 1, D_MODEL))


def _combine_kernel(n_first, d_ref, x1a_ref, x1b_ref, g_ref, nf_ref, eo_ref, ya_ref, yb_ref, buf, y_s, sem):
    i = pl.program_id(0)
    tl = ROUTE_TILE

    def gather(start):
        def body(j2, c):
            for u in range(DMA_UNROLL):
                j = j2 * DMA_UNROLL + u
                for k in range(TOP_K):
                    cp = _row_copy(eo_ref.at[d_ref[k, j]], buf.at[k, j], sem)
                    if start:
                        cp.start(priority=(u + k) % 2)
                    else:
                        cp.wait()
            return c

        lax.fori_loop(0, tl // DMA_UNROLL, body, 0)

    gather(True)
    sel = (lax.broadcasted_iota(I32, (8, LANE), 0) == lax.broadcasted_iota(I32, (8, LANE), 1)).astype(F32)
    gt = _dot_tn(g_ref[...], sel, precision=HI)
    gather(False)
    for k in range(TOP_K):
        for s, slab in enumerate(_load_rows(buf.at[k])):
            cols = slice(LANE * s, LANE * s + LANE)
            term = gt[:, k:k + 1] * slab
            y_s[:, cols] = term if k == 0 else y_s[:, cols] + term
    y = jnp.where(i < n_first, x1a_ref[...], x1b_ref[...]) + y_s[...]
    y = y * lax.rsqrt(jnp.mean(y * y, axis=-1, keepdims=True) + EPS) * nf_ref[...]

    @pl.when(i < n_first)
    def _():
        ya_ref[...] = y

    @pl.when(i >= n_first)
    def _():
        yb_ref[...] = y


def _combine(x1a, x1b, gates8, d3, eo3, nf):
    nt = (x1a.shape[0] + x1b.shape[0]) // ROUTE_TILE
    n_first = x1a.shape[0] // ROUTE_TILE
    map_a, map_b = _two_source_maps(n_first)
    row_specs = (pl.BlockSpec((ROUTE_TILE, D_MODEL), map_a), pl.BlockSpec((ROUTE_TILE, D_MODEL), map_b))
    grid_spec = pltpu.PrefetchScalarGridSpec(
        num_scalar_prefetch=0,
        grid=(nt,),
        in_specs=[pl.BlockSpec((None, TOP_K, ROUTE_TILE), lambda i: (i, 0, 0), memory_space=pltpu.SMEM),
                  *row_specs,
                  pl.BlockSpec((8, ROUTE_TILE), lambda i: (0, i)),
                  _const_spec(nf.shape),
                  pl.BlockSpec(memory_space=pl.ANY)],
        out_specs=row_specs,
        scratch_shapes=[pltpu.VMEM((TOP_K, ROUTE_TILE, ROW_TILES, LANE), F32),
                        pltpu.VMEM((ROUTE_TILE, D_MODEL), F32), pltpu.SemaphoreType.DMA],
    )
    return pl.pallas_call(
        functools.partial(_combine_kernel, n_first),
        grid_spec=grid_spec,
        out_shape=(jax.ShapeDtypeStruct(x1a.shape, F32), jax.ShapeDtypeStruct(x1b.shape, F32)),
        compiler_params=pltpu.CompilerParams(dimension_semantics=("arbitrary",), vmem_limit_bytes=VMEM_LIMIT),
        name="combine",
    )(d3, x1a, x1b, gates8, nf, eo3)


def _moe_and_norm(x1, xn2, lgt, wg, bg, wu, bu, wd, bd, nf):
    ta = x1[0].shape[0] + x1[1].shape[0]
    n_rows = (-(-(ta * TOP_K) // MOE_BLOCK) + N_EXPERTS) * MOE_BLOCK
    gates8, d3, cnt = _route(*lgt)
    counts = cnt[:, 0].astype(I32)
    padded = (counts + MOE_BLOCK - 1) // MOE_BLOCK * MOE_BLOCK
    pend = jnp.cumsum(padded)
    n_used = (pend[-1:] // MOE_BLOCK).astype(I32)
    block_row = jnp.arange(n_rows // MOE_BLOCK, dtype=I32) * MOE_BLOCK
    block_e = jnp.minimum(jnp.sum((pend[None, :] <= block_row[:, None]).astype(I32), axis=1), N_EXPERTS - 1)
    xs = _dispatch(*xn2, d3, pend.astype(I32), padded.astype(I32), n_used, n_rows)
    eo = _experts(xs, block_e, n_used, wg, bg, wu, bu, wd, bd)
    return _combine(*x1, gates8, d3, eo, nf)


def _prep_consts(norm1_w, w_in, conv_w, conv_b, dt_bias, A_log, D_skip, ssd_norm_w, hgrn_lower_bound, hgrn_norm_w,
                 w_out, norm2_w, w_router, b_router):
    row = lambda v: v.reshape(1, -1).astype(F32)
    pad_lane = lambda v: jnp.pad(v.reshape(1, -1).astype(F32), ((0, 0), (0, LANE - v.size)))
    wa = w_in[:, :Q0].astype(BF16)
    wq = w_in[:, Q0 + SSD_HEADS:].astype(BF16)
    wdt = jnp.pad(w_in[:, Q0:Q0 + SSD_HEADS], ((0, 0), (0, LANE - SSD_HEADS))).astype(BF16)
    dskip = jnp.repeat(D_skip.astype(F32), SSD_P).reshape(1, -1)
    return (row(norm1_w), wa, wq, wdt, conv_w.astype(F32), row(conv_b), pad_lane(dt_bias), pad_lane(A_log), dskip,
            row(ssd_norm_w), hgrn_lower_bound.astype(F32), row(hgrn_norm_w), w_out.astype(BF16), row(norm2_w),
            w_router.T.astype(F32), b_router.reshape(-1, 1).astype(F32))


def kernel(x_prompt, x_sample, state_conv, state_ssm, state_hgrn, norm1_w, w_in, conv_w, conv_b, dt_bias, A_log, D_skip, ssd_norm_w, hgrn_lower_bound, hgrn_norm_w, w_out, norm2_w, w_router, b_router, w_gate, b_gate, w_up, b_up, w_down, b_down, norm_f_w):
    depth = w_in.shape[0]
    assert depth == 1
    bsz, seq, _ = x_prompt.shape
    nseq, tdec, _ = x_sample.shape
    consts = _prep_consts(norm1_w[0], w_in[0], conv_w[0], conv_b[0], dt_bias[0], A_log[0], D_skip[0], ssd_norm_w[0],
                          hgrn_lower_bound, hgrn_norm_w[0], w_out[0], norm2_w[0], w_router[0], b_router[0])
    n1, wa, wq, wdt, cw, cb, dtb, alog, dskip, snw, hlb, hnw, wout, n2, wrt, br = consts
    chunk_consts = (cw, cb, dtb, alog, dskip, snw, hlb, hnw)

    x1_p, xn2_p, lg_p, conv_p, ssm_p, hgrn_p = _prompt_mixer(x_prompt, consts)

    xs_flat = x_sample.reshape(nseq * tdec, D_MODEL)
    proj_s = _sample_inproj(xs_flat, n1, (wa, wq, wdt))
    mix_s, conv_s, ssm_s, hgrn_s = _sample_step(
        proj_s, state_conv[0], state_ssm[0].reshape(nseq, SSD_HEADS * SSD_P, SSD_N),
        state_hgrn[0].reshape(nseq, HGRN_HEADS * HGRN_D, HGRN_D), chunk_consts)
    x1_s, xn2_s, lg_s = _sample_out(mix_s, xs_flat, wout, n2, wrt, br)

    y_p, y_s = _moe_and_norm((x1_p, x1_s), (xn2_p, xn2_s), (lg_p, lg_s), w_gate[0], b_gate[0], w_up[0], b_up[0],
                             w_down[0], b_down[0], norm_f_w.reshape(1, -1).astype(F32))
    y_prompt = y_p.reshape(bsz, seq, D_MODEL)
    y_sample = y_s.reshape(nseq, tdec, D_MODEL)
    return (y_prompt, y_sample,
            conv_p[None], ssm_p.reshape(1, bsz, SSD_HEADS, SSD_P, SSD_N),
            hgrn_p.reshape(1, bsz, HGRN_HEADS, HGRN_D, HGRN_D),
            conv_s[None], ssm_s.reshape(1, nseq, SSD_HEADS, SSD_P, SSD_N),
            hgrn_s.reshape(1, nseq, HGRN_HEADS, HGRN_D, HGRN_D))
```
